```python
import jax, jax.numpy as jnp
from jax import lax
import numpy as np

D_MODEL = 2048
BATCH = 2
SEQ = 16384
DEPTH = 4
DEC_BATCH = 32
DEC_SEQ = 32
PAST_LEN = 1024

CHUNK = 64
N_A_LAYERS = DEPTH // 2
N_B_LAYERS = DEPTH - N_A_LAYERS
N_DENSE = (DEPTH + 1) // 2
N_MOE = DEPTH // 2
CONV_WIDTH = 31
N_HEADS = 16
HEAD_DIM = D_MODEL // N_HEADS
D_FF = ((8 * D_MODEL // 3 + 255) // 256) * 256
N_EXPERTS = 8
TOP_K = 2
D_FF_EXPERT = D_MODEL // 2
Q_BLOCK = 128
K_BLOCK = 128
SCALE = HEAD_DIM ** -0.5
EPS = 1e-6

kernel_name = "yoco_conformer_stickbreaking_moe_step"


def rms_norm(x, g):
    xf = x.astype(jnp.float32)
    y = xf * lax.rsqrt(jnp.mean(xf * xf, axis=-1, keepdims=True) + EPS)
    return (y * g.astype(jnp.float32)).astype(x.dtype)


def layer_norm(x, g, b):
    xf = x.astype(jnp.float32)
    mu = jnp.mean(xf, axis=-1, keepdims=True)
    xc = xf - mu
    y = xc * lax.rsqrt(jnp.mean(xc * xc, axis=-1, keepdims=True) + EPS)
    return (y * g.astype(jnp.float32) + b.astype(jnp.float32)).astype(x.dtype)


def swiglu(x, wg, wu, wd):
    return (jax.nn.silu(x @ wg) * (x @ wu)) @ wd


def moe_swiglu(x, w_router, wg, wu, wd):
    shp = x.shape
    xt = x.reshape(-1, shp[-1])
    logits = jnp.matmul(xt, w_router, preferred_element_type=jnp.float32)
    top_val, top_idx = lax.top_k(logits, TOP_K)
    gates = jax.nn.softmax(top_val, axis=-1)
    combine = jnp.einsum("tk,tke->te", gates,
                         jax.nn.one_hot(top_idx, N_EXPERTS, dtype=jnp.float32)).astype(x.dtype)
    out = jnp.zeros_like(xt)
    for e in range(N_EXPERTS):
        out = out + combine[:, e:e + 1] * swiglu(xt, wg[e], wu[e], wd[e])
    return out.reshape(shp)


def conformer_conv(h, prev, w_in, b_in, w_dw, b_dw, ln_g, ln_b, w_out, b_out):
    u = h @ w_in + b_in
    a, g = jnp.split(u, 2, axis=-1)
    u = a * jax.nn.sigmoid(g)
    ext = jnp.concatenate([prev.astype(u.dtype), u], axis=1)
    c = lax.conv_general_dilated(ext, w_dw[:, None, :].astype(u.dtype), (1,), "VALID",
                                 dimension_numbers=("NWC", "WIO", "NWC"),
                                 feature_group_count=D_MODEL) + b_dw
    c = jax.nn.silu(layer_norm(c, ln_g, ln_b))
    return c @ w_out + b_out, ext[:, -(CONV_WIDTH - 1):]


def stick_breaking_block(q, kb, vb, q_pos, k_pos):
    nkb = kb.shape[1]
    z = jnp.einsum("bqhd,bnkhd->bhqnk", q, kb, preferred_element_type=jnp.float32) * SCALE
    mask = k_pos[None] < q_pos[:, None, None]
    ls = jax.nn.log_sigmoid(z)
    log_keep = jnp.where(mask, ls - z, 0.0)
    tri_k = jnp.tri(K_BLOCK, K_BLOCK, -1, dtype=jnp.float32)
    tri_n = jnp.tri(nkb, nkb, -1, dtype=jnp.float32)
    in_blk = jnp.einsum("bhqnk,kj->bhqnj", log_keep, tri_k)
    later = jnp.einsum("bhqn,nm->bhqm", jnp.sum(log_keep, axis=-1), tri_n)
    w = jnp.where(mask, jnp.exp(ls + in_blk + later[..., None]), 0.0)
    o = jnp.einsum("bhqnk,bnkhd->bqhd", w.astype(vb.dtype), vb, preferred_element_type=jnp.float32)
    return o.astype(q.dtype)


def stick_breaking_attention(q, k, v, past_len):
    B, Tq, H, Dh = q.shape
    Tk = k.shape[1]
    nkb = -(-Tk // K_BLOCK)
    pad = nkb * K_BLOCK - Tk
    if pad:
        k = jnp.pad(k, ((0, 0), (0, pad), (0, 0), (0, 0)))
        v = jnp.pad(v, ((0, 0), (0, pad), (0, 0), (0, 0)))
    kb = k.reshape(B, nkb, K_BLOCK, H, Dh)
    vb = v.reshape(B, nkb, K_BLOCK, H, Dh)
    k_pos = jnp.arange(nkb * K_BLOCK).reshape(nkb, K_BLOCK)
    if Tq <= Q_BLOCK:
        return stick_breaking_block(q, kb, vb, past_len + jnp.arange(Tq), k_pos)
    nb = Tq // Q_BLOCK
    qb = q.reshape(B, nb, Q_BLOCK, H, Dh).swapaxes(0, 1)
    starts = past_len + jnp.arange(nb) * Q_BLOCK
    ob = lax.map(lambda a: stick_breaking_block(a[0], kb, vb, a[1] + jnp.arange(Q_BLOCK), k_pos),
                 (qb, starts))
    return ob.swapaxes(0, 1).reshape(B, Tq, H, Dh)


def run_trunk(x, conv_prev, k_past, v_past,
              norm_mix, norm_ffn,
              conv_w_in, conv_b_in, conv_w_dw, conv_b_dw, conv_ln_g, conv_ln_b, conv_w_out, conv_b_out,
              kv_norm, w_kv, k_norm, w_q, q_norm, w_o,
              ffn_w_gate, ffn_w_up, ffn_w_down,
              moe_router, moe_w_gate, moe_w_up, moe_w_down):
    B, T, _ = x.shape
    past_len = 0 if k_past is None else k_past.shape[1]
    new_conv = []
    k_new = v_new = k_all = v_all = None
    for layer in range(DEPTH):
        if layer == N_A_LAYERS:
            hk = rms_norm(x, kv_norm)
            kv = (hk @ w_kv).reshape(B, T, 2, N_HEADS, HEAD_DIM)
            k_new = rms_norm(kv[:, :, 0], k_norm)
            v_new = kv[:, :, 1]
            if k_past is None:
                k_all, v_all = k_new, v_new
            else:
                k_all = jnp.concatenate([k_past.astype(k_new.dtype), k_new], axis=1)
                v_all = jnp.concatenate([v_past.astype(v_new.dtype), v_new], axis=1)
        h = rms_norm(x, norm_mix[layer])
        if layer < N_A_LAYERS:
            mix, st = conformer_conv(h, conv_prev[layer], conv_w_in[layer], conv_b_in[layer],
                                     conv_w_dw[layer], conv_b_dw[layer], conv_ln_g[layer],
                                     conv_ln_b[layer], conv_w_out[layer], conv_b_out[layer])
            new_conv.append(st)
        else:
            j = layer - N_A_LAYERS
            q = rms_norm((h @ w_q[j]).reshape(B, T, N_HEADS, HEAD_DIM), q_norm[j])
            o = stick_breaking_attention(q, k_all, v_all, past_len)
            mix = o.reshape(B, T, N_HEADS * HEAD_DIM) @ w_o[j]
        x = x + mix
        h = rms_norm(x, norm_ffn[layer])
        i = layer // 2
        if layer % 2 == 0:
            x = x + swiglu(h, ffn_w_gate[i], ffn_w_up[i], ffn_w_down[i])
        else:
            x = x + moe_swiglu(h, moe_router[i], moe_w_gate[i], moe_w_up[i], moe_w_down[i])
    return x, k_new, v_new, jnp.stack(new_conv)


def setup_inputs(seed: int = 0) -> dict:
    key = jax.random.key(seed)
    ks = jax.random.split(key, 32)
    f32 = jnp.float32

    def nrm(k, shape, scale):
        return jax.random.normal(k, shape, f32) * scale

    D, H, Dh, W = D_MODEL, N_HEADS, HEAD_DIM, CONV_WIDTH
    return {
        "x_prompt": nrm(ks[0], (BATCH, SEQ, D), 1.0),
        "x_sample": nrm(ks[1], (DEC_BATCH, DEC_SEQ, D), 1.0),
        "cache_k": nrm(ks[2], (DEC_BATCH, PAST_LEN, H, Dh), 1.0),
        "cache_v": nrm(ks[3], (DEC_BATCH, PAST_LEN, H, Dh), 1.0),
        "state_conv": nrm(ks[4], (N_A_LAYERS, DEC_BATCH, W - 1, D), 0.5),
        "norm_mix": 1.0 + nrm(ks[5], (DEPTH, D), 0.01),
        "norm_ffn": 1.0 + nrm(ks[6], (DEPTH, D), 0.01),
        "conv_w_in": nrm(ks[7], (N_A_LAYERS, D, 2 * D), D ** -0.5),
        "conv_b_in": nrm(ks[8], (N_A_LAYERS, 2 * D), 0.01),
        "conv_w_dw": nrm(ks[9], (N_A_LAYERS, W, D), W ** -0.5),
        "conv_b_dw": nrm(ks[10], (N_A_LAYERS, D), 0.01),
        "conv_ln_g": 1.0 + nrm(ks[11], (N_A_LAYERS, D), 0.01),
        "conv_ln_b": nrm(ks[12], (N_A_LAYERS, D), 0.01),
        "conv_w_out": nrm(ks[13], (N_A_LAYERS, D, D), D ** -0.5),
        "conv_b_out": nrm(ks[14], (N_A_LAYERS, D), 0.01),
        "kv_norm": 1.0 + nrm(ks[15], (D,), 0.01),
        "w_kv": nrm(ks[16], (D, 2 * H * Dh), D ** -0.5),
        "k_norm": 1.0 + nrm(ks[17], (Dh,), 0.01),
        "w_q": nrm(ks[18], (N_B_LAYERS, D, H * Dh), D ** -0.5),
        "q_norm": 1.0 + nrm(ks[19], (N_B_LAYERS, Dh), 0.01),
        "w_o": nrm(ks[20], (N_B_LAYERS, H * Dh, D), (H * Dh) ** -0.5),
        "ffn_w_gate": nrm(ks[21], (N_DENSE, D, D_FF), D ** -0.5),
        "ffn_w_up": nrm(ks[22], (N_DENSE, D, D_FF), D ** -0.5),
        "ffn_w_down": nrm(ks[23], (N_DENSE, D_FF, D), D_FF ** -0.5),
        "moe_router": nrm(ks[24], (N_MOE, D, N_EXPERTS), D ** -0.5),
        "moe_w_gate": nrm(ks[25], (N_MOE, N_EXPERTS, D, D_FF_EXPERT), D ** -0.5),
        "moe_w_up": nrm(ks[26], (N_MOE, N_EXPERTS, D, D_FF_EXPERT), D ** -0.5),
        "moe_w_down": nrm(ks[27], (N_MOE, N_EXPERTS, D_FF_EXPERT, D), D_FF_EXPERT ** -0.5),
    }


def reference(x_prompt, x_sample, cache_k, cache_v, state_conv,
              norm_mix, norm_ffn,
              conv_w_in, conv_b_in, conv_w_dw, conv_b_dw, conv_ln_g, conv_ln_b, conv_w_out, conv_b_out,
              kv_norm, w_kv, k_norm, w_q, q_norm, w_o,
              ffn_w_gate, ffn_w_up, ffn_w_down,
              moe_router, moe_w_gate, moe_w_up, moe_w_down):
    weights = (norm_mix, norm_ffn,
               conv_w_in, conv_b_in, conv_w_dw, conv_b_dw, conv_ln_g, conv_ln_b, conv_w_out, conv_b_out,
               kv_norm, w_kv, k_norm, w_q, q_norm, w_o,
               ffn_w_gate, ffn_w_up, ffn_w_down,
               moe_router, moe_w_gate, moe_w_up, moe_w_down)
    conv_zero = jnp.zeros((N_A_LAYERS, x_prompt.shape[0], CONV_WIDTH - 1, D_MODEL), x_prompt.dtype)
    y_prompt, k_prompt, v_prompt, conv_prompt = run_trunk(x_prompt, conv_zero, None, None, *weights)
    y_sample, k_sample, v_sample, conv_sample = run_trunk(x_sample, state_conv, cache_k, cache_v, *weights)
    return (y_prompt, y_sample, k_prompt, v_prompt, conv_prompt, k_sample, v_sample, conv_sample)
```

```python
import functools

import jax
import jax.numpy as jnp
from jax import lax
from jax.experimental import pallas as pl
from jax.experimental.pallas import tpu as pltpu

F32 = jnp.float32
BF16 = jnp.bfloat16

EPS = 1e-6
LANES = 128
HALO = 32
VMEM_LIMIT = 56 * 1024 * 1024
LOG_ZERO = -104.0


def _params(*sem):
    return pltpu.CompilerParams(dimension_semantics=sem, vmem_limit_bytes=VMEM_LIMIT)


def _pick(n, pref, mult=8):
    if n <= pref:
        return n
    t = (pref // mult) * mult
    while t > mult and n % t:
        t -= mult
    assert n % t == 0, (n, pref, mult)
    return t


def _rms_bf16(x, g):
    ms = jnp.mean(x * x, axis=-1, keepdims=True)
    return (x * lax.rsqrt(ms + EPS) * g).astype(BF16)


def _dot(a, b):
    return jnp.dot(a, b, preferred_element_type=F32)


def _glu_in_kernel(x_ref, g_ref, wa_ref, wg_ref, ba_ref, bg_ref, u_ref, h_scr):
    @pl.when(pl.program_id(1) == 0)
    def _():
        h_scr[...] = _rms_bf16(x_ref[...], g_ref[...])

    h = h_scr[...]
    a = _dot(h, wa_ref[...]) + ba_ref[...]
    g = _dot(h, wg_ref[...]) + bg_ref[...]
    u_ref[...] = a * jax.nn.sigmoid(g)


def _glu_in(x, gain, wa, wg, ba, bg):
    m, d = x.shape
    n = wa.shape[1]
    tm, tn = _pick(m, 1024), _pick(n, 512, LANES)
    return pl.pallas_call(
        _glu_in_kernel,
        grid=(m // tm, n // tn),
        in_specs=[
            pl.BlockSpec((tm, d), lambda i, j: (i, 0)),
            pl.BlockSpec((1, d), lambda i, j: (0, 0)),
            pl.BlockSpec((d, tn), lambda i, j: (0, j)),
            pl.BlockSpec((d, tn), lambda i, j: (0, j)),
            pl.BlockSpec((1, tn), lambda i, j: (0, j)),
            pl.BlockSpec((1, tn), lambda i, j: (0, j)),
        ],
        out_specs=pl.BlockSpec((tm, tn), lambda i, j: (i, j)),
        out_shape=jax.ShapeDtypeStruct((m, n), F32),
        scratch_shapes=[pltpu.VMEM((tm, d), BF16)],
        compiler_params=_params("parallel", "arbitrary"),
        name="glu_in",
    )(x, gain, wa, wg, ba, bg)


def _conv_kernel(u_ref, halo_ref, prev_ref, wdw_ref, bdw_ref, lng_ref, lnb_ref, c_ref,
                 ext_scr, sh_scr, acc_scr, *, tc, width, rows_ln):
    d = u_ref.shape[1]
    i = pl.program_id(1)

    @pl.when(i == 0)
    def _():
        ext_scr[0:HALO, :] = prev_ref[0]

    @pl.when(i > 0)
    def _():
        ext_scr[0:HALO, :] = halo_ref[...]

    ext_scr[HALO:HALO + tc, :] = u_ref[...]

    off = HALO - (width - 1)
    n_shift = 8
    rc = min(tc, 64)

    def lane_body(c, carry):
        l0 = pl.multiple_of(c * LANES, LANES)
        for r in range(n_shift):
            taps_r = [w for w in range(width) if w % n_shift == r]
            n_rows = tc + (max(taps_r) - r)
            sh_scr[r, 0:n_rows, :] = ext_scr[off + r:off + r + n_rows, pl.ds(l0, LANES)]
        for r0 in range(0, tc, rc):
            acc = jnp.broadcast_to(bdw_ref[:, pl.ds(l0, LANES)], (rc, LANES))
            for w in range(width):
                r, q = w % n_shift, w // n_shift
                win = sh_scr[r, r0 + 8 * q:r0 + 8 * q + rc, :]
                acc = acc + win * wdw_ref[w:w + 1, pl.ds(l0, LANES)]
            acc_scr[r0:r0 + rc, pl.ds(l0, LANES)] = acc
        return carry

    lax.fori_loop(0, d // LANES, lane_body, 0)

    def ln_body(r, carry):
        r0 = pl.multiple_of(r * rows_ln, rows_ln)
        c = acc_scr[pl.ds(r0, rows_ln), :]
        mu = jnp.mean(c, axis=-1, keepdims=True)
        xc = c - mu
        var = jnp.mean(xc * xc, axis=-1, keepdims=True)
        y = xc * lax.rsqrt(var + EPS) * lng_ref[...] + lnb_ref[...]
        c_ref[pl.ds(r0, rows_ln), :] = (y * jax.nn.sigmoid(y)).astype(c_ref.dtype)
        return carry

    lax.fori_loop(0, tc // rows_ln, ln_body, 0)


def _conv_ln_swish(u, prev, wdw, bdw, lng, lnb, batch, seq):
    m, d = u.shape
    width = wdw.shape[0]
    assert width - 1 <= HALO and seq % HALO == 0
    tc = _pick(seq, 256, HALO)
    nt = seq // tc
    hb = tc // HALO
    wdw_p = jnp.pad(wdw, ((0, (-width) % 8), (0, 0)))
    kern = functools.partial(_conv_kernel, tc=tc, width=width, rows_ln=16)
    return pl.pallas_call(
        kern,
        grid=(batch, nt),
        in_specs=[
            pl.BlockSpec((tc, d), lambda b, i: (b * nt + i, 0)),
            pl.BlockSpec((HALO, d), lambda b, i: (jnp.maximum((b * nt + i) * hb - 1, 0), 0)),
            pl.BlockSpec((1, HALO, d), lambda b, i: (b, 0, 0)),
            pl.BlockSpec(wdw_p.shape, lambda b, i: (0, 0)),
            pl.BlockSpec((1, d), lambda b, i: (0, 0)),
            pl.BlockSpec((1, d), lambda b, i: (0, 0)),
            pl.BlockSpec((1, d), lambda b, i: (0, 0)),
        ],
        out_specs=pl.BlockSpec((tc, d), lambda b, i: (b * nt + i, 0)),
        out_shape=jax.ShapeDtypeStruct((m, d), BF16),
        scratch_shapes=[
            pltpu.VMEM((HALO + tc, d), F32),
            pltpu.VMEM((8, tc + 24, LANES), F32),
            pltpu.VMEM((tc, d), F32),
        ],
        compiler_params=_params("parallel", "arbitrary"),
        name="conv_ln_swish",
    )(u, u, prev, wdw_p, bdw, lng, lnb)


def _mm_res_kernel(x_ref, w_ref, b_ref, r_ref, o_ref):
    o_ref[...] = r_ref[...] + b_ref[...] + _dot(x_ref[...], w_ref[...])


def _mm_res(x, w, b, res):
    m, k = x.shape
    n = w.shape[1]
    tm, tn = _pick(m, 1024), _pick(n, 512, LANES)
    return pl.pallas_call(
        _mm_res_kernel,
        grid=(m // tm, n // tn),
        in_specs=[
            pl.BlockSpec((tm, k), lambda i, j: (i, 0)),
            pl.BlockSpec((k, tn), lambda i, j: (0, j)),
            pl.BlockSpec((1, tn), lambda i, j: (0, j)),
            pl.BlockSpec((tm, tn), lambda i, j: (i, j)),
        ],
        out_specs=pl.BlockSpec((tm, tn), lambda i, j: (i, j)),
        out_shape=jax.ShapeDtypeStruct((m, n), F32),
        compiler_params=_params("parallel", "parallel"),
        name="mm_res",
    )(x, w, b, res)


def _head_rms(y, gain, scale):
    out = []
    for c in range(y.shape[1] // LANES):
        yc = y[:, c * LANES:(c + 1) * LANES]
        ms = jnp.mean(yc * yc, axis=-1, keepdims=True)
        yc = yc * lax.rsqrt(ms + EPS) * gain
        out.append(yc * scale if scale != 1.0 else yc)
    return out


def _q_kernel(x_ref, g_ref, w_ref, qn_ref, q_ref, h_scr, *, scale):
    @pl.when(pl.program_id(1) == 0)
    def _():
        h_scr[...] = _rms_bf16(x_ref[...], g_ref[...])

    y = _dot(h_scr[...], w_ref[...])
    for c, yc in enumerate(_head_rms(y, qn_ref[...], scale)):
        q_ref[:, c * LANES:(c + 1) * LANES] = yc.astype(q_ref.dtype)


def _q_proj(x, gain, w, qn, scale):
    m, d = x.shape
    n = w.shape[1]
    tm, tn = _pick(m, 1024), _pick(n, 512, LANES)
    return pl.pallas_call(
        functools.partial(_q_kernel, scale=scale),
        grid=(m // tm, n // tn),
        in_specs=[
            pl.BlockSpec((tm, d), lambda i, j: (i, 0)),
            pl.BlockSpec((1, d), lambda i, j: (0, 0)),
            pl.BlockSpec((d, tn), lambda i, j: (0, j)),
            pl.BlockSpec((1, LANES), lambda i, j: (0, 0)),
        ],
        out_specs=pl.BlockSpec((tm, tn), lambda i, j: (i, j)),
        out_shape=jax.ShapeDtypeStruct((m, n), BF16),
        scratch_shapes=[pltpu.VMEM((tm, d), BF16)],
        compiler_params=_params("parallel", "arbitrary"),
        name="q_proj",
    )(x, gain, w, qn)


def _kv_kernel(x_ref, g_ref, wk_ref, wv_ref, kn_ref, k_ref, v_ref, kb_ref, vb_ref, h_scr):
    @pl.when(pl.program_id(1) == 0)
    def _():
        h_scr[...] = _rms_bf16(x_ref[...], g_ref[...])

    h = h_scr[...]
    k = _dot(h, wk_ref[...])
    for c, kc in enumerate(_head_rms(k, kn_ref[...], 1.0)):
        k_ref[:, c * LANES:(c + 1) * LANES] = kc
        kb_ref[:, c * LANES:(c + 1) * LANES] = kc.astype(BF16)
    v = _dot(h, wv_ref[...])
    v_ref[...] = v
    vb_ref[...] = v.astype(BF16)


def _kv_proj(x, gain, wk, wv, kn):
    m, d = x.shape
    n = wk.shape[1]
    tm, tn = _pick(m, 1024), _pick(n, 512, LANES)
    blk = pl.BlockSpec((tm, tn), lambda i, j: (i, j))
    return pl.pallas_call(
        _kv_kernel,
        grid=(m // tm, n // tn),
        in_specs=[
            pl.BlockSpec((tm, d), lambda i, j: (i, 0)),
            pl.BlockSpec((1, d), lambda i, j: (0, 0)),
            pl.BlockSpec((d, tn), lambda i, j: (0, j)),
            pl.BlockSpec((d, tn), lambda i, j: (0, j)),
            pl.BlockSpec((1, LANES), lambda i, j: (0, 0)),
        ],
        out_specs=[blk, blk, blk, blk],
        out_shape=[jax.ShapeDtypeStruct((m, n), F32), jax.ShapeDtypeStruct((m, n), F32),
                   jax.ShapeDtypeStruct((m, n), BF16), jax.ShapeDtypeStruct((m, n), BF16)],
        scratch_shapes=[pltpu.VMEM((tm, d), BF16)],
        compiler_params=_params("parallel", "arbitrary"),
        name="kv_proj",
    )(x, gain, wk, wv, kn)


def _strict_lower(n):
    row = lax.broadcasted_iota(jnp.int32, (n, n), 0)
    col = lax.broadcasted_iota(jnp.int32, (n, n), 1)
    return col < row


def _sb_block(q, k, v, tri, later, mask):
    z = lax.dot_general(q, k, (((1,), (1,)), ((), ())), preferred_element_type=F32)
    l1p = jnp.log(1.0 + jnp.exp(-jnp.abs(z)))
    log_keep = -(jnp.maximum(z, 0.0) + l1p)
    log_beta = jnp.minimum(z, 0.0) - l1p
    if mask is not None:
        log_keep = jnp.where(mask, log_keep, 0.0)
    in_blk = _dot(log_keep.astype(BF16), tri)
    w = jnp.exp(log_beta + in_blk + later)
    if mask is not None:
        w = jnp.where(mask, w, 0.0)
    o = _dot(w.astype(BF16), v)
    return o, later + jnp.sum(log_keep, axis=-1, keepdims=True)


def _sweep_earlier(q, load_kv, tri, n_blocks, o, later):
    def cond(c):
        kb, _, later = c
        return jnp.logical_and(kb >= 0, jnp.max(later) > LOG_ZERO)

    def body(c):
        kb, o, later = c
        k, v = load_kv(kb)
        do, later = _sb_block(q, k, v, tri, later, None)
        return kb - 1, o + do, later

    _, o, _ = lax.while_loop(cond, body, (n_blocks - 1, o, later))
    return o


def _attn_self_kernel(q_ref, k_ref, v_ref, o_ref, *, bq):
    i = pl.program_id(2)
    q = q_ref[...]
    tri = jnp.where(_strict_lower(bq), 1.0, 0.0).astype(BF16)
    r0 = pl.multiple_of(i * bq, bq)
    o, later = _sb_block(q, k_ref[pl.ds(r0, bq), :], v_ref[pl.ds(r0, bq), :], tri,
                         jnp.zeros((bq, 1), F32), _strict_lower(bq))

    def load_kv(kb):
        k0 = pl.multiple_of(kb * bq, bq)
        return k_ref[pl.ds(k0, bq), :], v_ref[pl.ds(k0, bq), :]

    o = _sweep_earlier(q, load_kv, tri, i, o, later)
    o_ref[...] = o.astype(o_ref.dtype)


def _attn_self(q, k, v, batch, seq):
    m, hd = q.shape
    heads = hd // LANES
    bq = _pick(seq, 256, LANES)
    nq = seq // bq
    return pl.pallas_call(
        functools.partial(_attn_self_kernel, bq=bq),
        grid=(batch, heads, nq),
        in_specs=[
            pl.BlockSpec((bq, LANES), lambda b, h, i: (b * nq + i, h)),
            pl.BlockSpec((seq, LANES), lambda b, h, i: (b, h)),
            pl.BlockSpec((seq, LANES), lambda b, h, i: (b, h)),
        ],
        out_specs=pl.BlockSpec((bq, LANES), lambda b, h, i: (b * nq + i, h)),
        out_shape=jax.ShapeDtypeStruct((m, hd), BF16),
        compiler_params=_params("parallel", "parallel", "arbitrary"),
        name="attn_self",
    )(q, k, v)


def _attn_cached_kernel(q_ref, kn_ref, vn_ref, ck_ref, cv_ref, o_ref, *, bk):
    tq = q_ref.shape[0]
    past = ck_ref.shape[1]
    q = q_ref[...]
    tri = jnp.where(_strict_lower(bk), 1.0, 0.0).astype(BF16)
    o, later = _sb_block(q, kn_ref[...], vn_ref[...], tri[:tq, :tq],
                         jnp.zeros((tq, 1), F32), _strict_lower(tq))

    def load_kv(kb):
        k0 = pl.multiple_of(kb * bk, bk)
        return ck_ref[0, pl.ds(k0, bk), :].astype(BF16), cv_ref[0, pl.ds(k0, bk), :].astype(BF16)

    o = _sweep_earlier(q, load_kv, tri, past // bk, o, later)
    o_ref[...] = o.astype(o_ref.dtype)


def _attn_cached(q, k_new, v_new, cache_k, cache_v, batch, seq):
    m, hd = q.shape
    heads = hd // LANES
    past = cache_k.shape[1]
    bk = _pick(past, 256, LANES)
    assert seq <= bk and seq % 16 == 0
    new = pl.BlockSpec((seq, LANES), lambda b, h: (b, h))
    old = pl.BlockSpec((1, past, LANES), lambda b, h: (b, 0, h))
    return pl.pallas_call(
        functools.partial(_attn_cached_kernel, bk=bk),
        grid=(batch, heads),
        in_specs=[new, new, new, old, old],
        out_specs=new,
        out_shape=jax.ShapeDtypeStruct((m, hd), BF16),
        compiler_params=_params("parallel", "parallel"),
        name="attn_cached",
    )(q, k_new, v_new, cache_k, cache_v)


def _ffn_kernel(x_ref, g_ref, wg_ref, wu_ref, wd_ref, o_ref, h_scr):
    @pl.when(pl.program_id(1) == 0)
    def _():
        x = x_ref[...]
        h_scr[...] = _rms_bf16(x, g_ref[...])
        o_ref[...] = x

    h = h_scr[...]
    act = jax.nn.silu(_dot(h, wg_ref[...])) * _dot(h, wu_ref[...])
    o_ref[...] += _dot(act.astype(BF16), wd_ref[...])


def _ffn(x, gain, wg, wu, wd):
    m, d = x.shape
    f = wg.shape[1]
    tm, tf = _pick(m, 512), _pick(f, 512, LANES)
    return pl.pallas_call(
        _ffn_kernel,
        grid=(m // tm, f // tf),
        in_specs=[
            pl.BlockSpec((tm, d), lambda i, j: (i, 0)),
            pl.BlockSpec((1, d), lambda i, j: (0, 0)),
            pl.BlockSpec((d, tf), lambda i, j: (0, j)),
            pl.BlockSpec((d, tf), lambda i, j: (0, j)),
            pl.BlockSpec((tf, d), lambda i, j: (j, 0)),
        ],
        out_specs=pl.BlockSpec((tm, d), lambda i, j: (i, 0)),
        out_shape=jax.ShapeDtypeStruct((m, d), F32),
        scratch_shapes=[pltpu.VMEM((tm, d), BF16)],
        compiler_params=_params("parallel", "arbitrary"),
        name="ffn",
    )(x, gain, wg, wu, wd)


def _split_bf16(a):
    hi = a.astype(BF16)
    return hi, (a - hi.astype(F32)).astype(BF16)


def _router_kernel(x_ref, g_ref, wr_ref, c_ref, *, n_experts):
    x = x_ref[...]
    ms = jnp.mean(x * x, axis=-1, keepdims=True)
    h = x * lax.rsqrt(ms + EPS) * g_ref[...]
    h_hi, h_lo = _split_bf16(h)
    w_hi, w_lo = _split_bf16(wr_ref[...])
    logits = _dot(h_hi, w_hi) + (_dot(h_hi, w_lo) + _dot(h_lo, w_hi))
    lane = lax.broadcasted_iota(jnp.int32, logits.shape, 1).astype(F32)
    neg = -jnp.inf
    l1 = jnp.where(lane < n_experts, logits, neg)
    m1 = jnp.max(l1, axis=-1, keepdims=True)
    i1 = jnp.min(jnp.where(l1 == m1, lane, float(LANES)), axis=-1, keepdims=True)
    l2 = jnp.where(lane == i1, neg, l1)
    m2 = jnp.max(l2, axis=-1, keepdims=True)
    i2 = jnp.min(jnp.where(l2 == m2, lane, float(LANES)), axis=-1, keepdims=True)
    e2 = jnp.exp(m2 - m1)
    den = 1.0 + e2
    c_ref[...] = jnp.where(lane == i1, 1.0 / den, 0.0) + jnp.where(lane == i2, e2 / den, 0.0)


def _router(x, gain, wr):
    m, d = x.shape
    n_experts = wr.shape[1]
    assert n_experts <= LANES
    wr_p = jnp.pad(wr, ((0, 0), (0, LANES - n_experts)))
    tm = _pick(m, 512)
    return pl.pallas_call(
        functools.partial(_router_kernel, n_experts=n_experts),
        grid=(m // tm,),
        in_specs=[
            pl.BlockSpec((tm, d), lambda i: (i, 0)),
            pl.BlockSpec((1, d), lambda i: (0, 0)),
            pl.BlockSpec((d, LANES), lambda i: (0, 0)),
        ],
        out_specs=pl.BlockSpec((tm, LANES), lambda i: (i, 0)),
        out_shape=jax.ShapeDtypeStruct((m, LANES), F32),
        compiler_params=_params("parallel"),
        name="router",
    )(x, gain, wr_p)


def _moe_kernel(x_ref, g_ref, c_ref, wg_ref, wu_ref, wd_ref, o_ref, h_scr):
    e = pl.program_id(1)

    @pl.when(jnp.logical_and(e == 0, pl.program_id(2) == 0))
    def _():
        x = x_ref[...]
        h_scr[...] = _rms_bf16(x, g_ref[...])
        o_ref[...] = x

    h = h_scr[...]
    c = c_ref[...]
    lane = lax.broadcasted_iota(jnp.int32, c.shape, 1)
    ce = jnp.sum(jnp.where(lane == e, c, 0.0), axis=-1, keepdims=True)
    act = jax.nn.silu(_dot(h, wg_ref[0])) * _dot(h, wu_ref[0]) * ce
    o_ref[...] += _dot(act.astype(BF16), wd_ref[0])


def _moe(x, gain, combine, wg, wu, wd):
    m, d = x.shape
    n_experts, _, f = wg.shape
    tm, tf = _pick(m, 512), _pick(f, 512, LANES)
    return pl.pallas_call(
        _moe_kernel,
        grid=(m // tm, n_experts, f // tf),
        in_specs=[
            pl.BlockSpec((tm, d), lambda i, e, j: (i, 0)),
            pl.BlockSpec((1, d), lambda i, e, j: (0, 0)),
            pl.BlockSpec((tm, LANES), lambda i, e, j: (i, 0)),
            pl.BlockSpec((1, d, tf), lambda i, e, j: (e, 0, j)),
            pl.BlockSpec((1, d, tf), lambda i, e, j: (e, 0, j)),
            pl.BlockSpec((1, tf, d), lambda i, e, j: (e, j, 0)),
        ],
        out_specs=pl.BlockSpec((tm, d), lambda i, e, j: (i, 0)),
        out_shape=jax.ShapeDtypeStruct((m, d), F32),
        scratch_shapes=[pltpu.VMEM((tm, d), BF16)],
        compiler_params=_params("parallel", "arbitrary", "arbitrary"),
        name="moe",
    )(x, gain, combine, wg, wu, wd)


def _trunk(x, batch, seq, conv_prev, cache, w):
    d = x.shape[1]
    depth = w["norm_mix"].shape[0]
    n_conv = w["conv_w_dw"].shape[0]
    width = w["conv_w_dw"].shape[1]
    scale = float(LANES) ** -0.5
    states = []
    k_f32 = v_f32 = k_b = v_b = None
    for layer in range(depth):
        if layer == n_conv:
            k_f32, v_f32, k_b, v_b = _kv_proj(x, w["kv_norm"], w["wk"], w["wv"], w["k_norm"])
        gain = w["norm_mix"][layer:layer + 1]
        if layer < n_conv:
            u = _glu_in(x, gain, w["conv_wa"][layer], w["conv_wg"][layer],
                        w["conv_ba"][layer], w["conv_bg"][layer])
            prev = conv_prev[layer]
            u3 = u.reshape(batch, seq, d)
            keep = width - 1
            if seq >= keep:
                states.append(u3[:, seq - keep:])
            else:
                states.append(jnp.concatenate([prev[:, seq:], u3], axis=1))
            prev_p = jnp.pad(prev, ((0, 0), (HALO - keep, 0), (0, 0)))
            c = _conv_ln_swish(u, prev_p, w["conv_w_dw"][layer], w["conv_b_dw"][layer:layer + 1],
                               w["conv_ln_g"][layer:layer + 1], w["conv_ln_b"][layer:layer + 1], batch, seq)
            x = _mm_res(c, w["conv_w_out"][layer], w["conv_b_out"][layer:layer + 1], x)
        else:
            j = layer - n_conv
            q = _q_proj(x, gain, w["w_q"][j], w["q_norm"][j:j + 1], scale)
            if cache is None:
                o = _attn_self(q, k_b, v_b, batch, seq)
            else:
                o = _attn_cached(q, k_b, v_b, cache[0], cache[1], batch, seq)
            x = _mm_res(o, w["w_o"][j], jnp.zeros((1, d), F32), x)
        gain = w["norm_ffn"][layer:layer + 1]
        i = layer // 2
        if layer % 2 == 0:
            x = _ffn(x, gain, w["ffn_w_gate"][i], w["ffn_w_up"][i], w["ffn_w_down"][i])
        else:
            combine = _router(x, gain, w["moe_router"][i])
            x = _moe(x, gain, combine, w["moe_w_gate"][i], w["moe_w_up"][i], w["moe_w_down"][i])
    return x, k_f32, v_f32, jnp.stack(states)


def kernel(x_prompt, x_sample, cache_k, cache_v, state_conv, norm_mix, norm_ffn, conv_w_in, conv_b_in, conv_w_dw, conv_b_dw, conv_ln_g, conv_ln_b, conv_w_out, conv_b_out, kv_norm, w_kv, k_norm, w_q, q_norm, w_o, ffn_w_gate, ffn_w_up, ffn_w_down, moe_router, moe_w_gate, moe_w_up, moe_w_down):
    d = x_prompt.shape[-1]
    heads = d // LANES
    assert k_norm.shape[0] == LANES and w_kv.shape[1] == 2 * d
    w = {
        "norm_mix": norm_mix, "norm_ffn": norm_ffn,
        "conv_wa": conv_w_in[:, :, :d].astype(BF16), "conv_wg": conv_w_in[:, :, d:].astype(BF16),
        "conv_ba": conv_b_in[:, None, :d], "conv_bg": conv_b_in[:, None, d:],
        "conv_w_dw": conv_w_dw, "conv_b_dw": conv_b_dw, "conv_ln_g": conv_ln_g, "conv_ln_b": conv_ln_b,
        "conv_w_out": conv_w_out.astype(BF16), "conv_b_out": conv_b_out,
        "kv_norm": kv_norm[None], "wk": w_kv[:, :d].astype(BF16), "wv": w_kv[:, d:].astype(BF16),
        "k_norm": k_norm[None], "w_q": w_q.astype(BF16), "q_norm": q_norm, "w_o": w_o.astype(BF16),
        "ffn_w_gate": ffn_w_gate.astype(BF16), "ffn_w_up": ffn_w_up.astype(BF16),
        "ffn_w_down": ffn_w_down.astype(BF16),
        "moe_router": moe_router, "moe_w_gate": moe_w_gate.astype(BF16),
        "moe_w_up": moe_w_up.astype(BF16), "moe_w_down": moe_w_down.astype(BF16),
    }
    n_conv, width = conv_w_dw.shape[0], conv_w_dw.shape[1]

    bp, sp, _ = x_prompt.shape
    zero_prev = jnp.zeros((n_conv, bp, width - 1, d), x_prompt.dtype)
    y_p, k_p, v_p, conv_p = _trunk(x_prompt.reshape(bp * sp, d), bp, sp, zero_prev, None, w)

    bs, ss, _ = x_sample.shape
    past = cache_k.shape[1]
    cache = (cache_k.reshape(bs, past, d), cache_v.reshape(bs, past, d))
    y_s, k_s, v_s, conv_s = _trunk(x_sample.reshape(bs * ss, d), bs, ss, state_conv, cache, w)

    return (y_p.reshape(bp, sp, d), y_s.reshape(bs, ss, d),
            k_p.reshape(bp, sp, heads, LANES), v_p.reshape(bp, sp, heads, LANES), conv_p,
            k_s.reshape(bs, ss, heads, LANES), v_s.reshape(bs, ss, heads, LANES), conv_s)
```

```python
import functools

import jax
import jax.numpy as jnp
from jax import lax
from jax.experimental import pallas as pl
from jax.experimental.pallas import tpu as pltpu

F32 = jnp.float32
BF16 = jnp.bfloat16

EPS = 1e-6
LANES = 128
HALO = 32
SMEM_CHUNK = 1024
VMEM_LIMIT = 56 * 1024 * 1024
LOG_ZERO = -104.0


def _params(*sem):
    return pltpu.CompilerParams(dimension_semantics=sem, vmem_limit_bytes=VMEM_LIMIT)


def _pick(n, pref, mult=8):
    if n <= pref:
        return n
    t = (pref // mult) * mult
    while t > mult and n % t:
        t -= mult
    assert n % t == 0, (n, pref, mult)
    return t


def _rms_bf16(x, g):
    ms = jnp.mean(x * x, axis=-1, keepdims=True)
    return (x * lax.rsqrt(ms + EPS) * g).astype(BF16)


def _dot(a, b):
    return jnp.dot(a, b, preferred_element_type=F32)


def _glu_in_kernel(x_ref, g_ref, wa_ref, wg_ref, ba_ref, bg_ref, u_ref, h_scr):
    @pl.when(pl.program_id(1) == 0)
    def _():
        h_scr[...] = _rms_bf16(x_ref[...], g_ref[...])

    h = h_scr[...]
    a = _dot(h, wa_ref[...]) + ba_ref[...]
    g = _dot(h, wg_ref[...]) + bg_ref[...]
    u_ref[...] = a * jax.nn.sigmoid(g)


def _glu_in(x, gain, wa, wg, ba, bg):
    m, d = x.shape
    n = wa.shape[1]
    tm, tn = _pick(m, 1024), _pick(n, 512, LANES)
    return pl.pallas_call(
        _glu_in_kernel,
        grid=(m // tm, n // tn),
        in_specs=[
            pl.BlockSpec((tm, d), lambda i, j: (i, 0)),
            pl.BlockSpec((1, d), lambda i, j: (0, 0)),
            pl.BlockSpec((d, tn), lambda i, j: (0, j)),
            pl.BlockSpec((d, tn), lambda i, j: (0, j)),
            pl.BlockSpec((1, tn), lambda i, j: (0, j)),
            pl.BlockSpec((1, tn), lambda i, j: (0, j)),
        ],
        out_specs=pl.BlockSpec((tm, tn), lambda i, j: (i, j)),
        out_shape=jax.ShapeDtypeStruct((m, n), F32),
        scratch_shapes=[pltpu.VMEM((tm, d), BF16)],
        compiler_params=_params("parallel", "arbitrary"),
        name="glu_in",
    )(x, gain, wa, wg, ba, bg)


def _conv_kernel(u_ref, halo_ref, prev_ref, wdw_ref, bdw_ref, lng_ref, lnb_ref, c_ref,
                 ext_scr, sh_scr, acc_scr, *, tc, width, rows_ln):
    d = u_ref.shape[1]
    i = pl.program_id(1)

    @pl.when(i == 0)
    def _():
        ext_scr[0:HALO, :] = prev_ref[0]

    @pl.when(i > 0)
    def _():
        ext_scr[0:HALO, :] = halo_ref[...]

    ext_scr[HALO:HALO + tc, :] = u_ref[...]

    off = HALO - (width - 1)
    n_shift = 8
    rc = min(tc, 64)

    def lane_body(c, carry):
        l0 = pl.multiple_of(c * LANES, LANES)
        for r in range(n_shift):
            taps_r = [w for w in range(width) if w % n_shift == r]
            n_rows = tc + (max(taps_r) - r)
            sh_scr[r, 0:n_rows, :] = ext_scr[off + r:off + r + n_rows, pl.ds(l0, LANES)]
        for r0 in range(0, tc, rc):
            acc = jnp.broadcast_to(bdw_ref[:, pl.ds(l0, LANES)], (rc, LANES))
            for w in range(width):
                r, q = w % n_shift, w // n_shift
                win = sh_scr[r, r0 + 8 * q:r0 + 8 * q + rc, :]
                acc = acc + win * wdw_ref[w:w + 1, pl.ds(l0, LANES)]
            acc_scr[r0:r0 + rc, pl.ds(l0, LANES)] = acc
        return carry

    lax.fori_loop(0, d // LANES, lane_body, 0)

    def ln_body(r, carry):
        r0 = pl.multiple_of(r * rows_ln, rows_ln)
        c = acc_scr[pl.ds(r0, rows_ln), :]
        mu = jnp.mean(c, axis=-1, keepdims=True)
        xc = c - mu
        var = jnp.mean(xc * xc, axis=-1, keepdims=True)
        y = xc * lax.rsqrt(var + EPS) * lng_ref[...] + lnb_ref[...]
        c_ref[pl.ds(r0, rows_ln), :] = (y * jax.nn.sigmoid(y)).astype(c_ref.dtype)
        return carry

    lax.fori_loop(0, tc // rows_ln, ln_body, 0)


def _conv_ln_swish(u, prev, wdw, bdw, lng, lnb, batch, seq):
    m, d = u.shape
    width = wdw.shape[0]
    assert width - 1 <= HALO and seq % HALO == 0
    tc = _pick(seq, 256, HALO)
    nt = seq // tc
    hb = tc // HALO
    wdw_p = jnp.pad(wdw, ((0, (-width) % 8), (0, 0)))
    kern = functools.partial(_conv_kernel, tc=tc, width=width, rows_ln=16)
    return pl.pallas_call(
        kern,
        grid=(batch, nt),
        in_specs=[
            pl.BlockSpec((tc, d), lambda b, i: (b * nt + i, 0)),
            pl.BlockSpec((HALO, d), lambda b, i: (jnp.maximum((b * nt + i) * hb - 1, 0), 0)),
            pl.BlockSpec((1, HALO, d), lambda b, i: (b, 0, 0)),
            pl.BlockSpec(wdw_p.shape, lambda b, i: (0, 0)),
            pl.BlockSpec((1, d), lambda b, i: (0, 0)),
            pl.BlockSpec((1, d), lambda b, i: (0, 0)),
            pl.BlockSpec((1, d), lambda b, i: (0, 0)),
        ],
        out_specs=pl.BlockSpec((tc, d), lambda b, i: (b * nt + i, 0)),
        out_shape=jax.ShapeDtypeStruct((m, d), BF16),
        scratch_shapes=[
            pltpu.VMEM((HALO + tc, d), F32),
            pltpu.VMEM((8, tc + 24, LANES), F32),
            pltpu.VMEM((tc, d), F32),
        ],
        compiler_params=_params("parallel", "arbitrary"),
        name="conv_ln_swish",
    )(u, u, prev, wdw_p, bdw, lng, lnb)


def _mm_res_kernel(x_ref, w_ref, b_ref, r_ref, o_ref):
    o_ref[...] = r_ref[...] + b_ref[...] + _dot(x_ref[...], w_ref[...])


def _mm_res(x, w, b, res):
    m, k = x.shape
    n = w.shape[1]
    tm, tn = _pick(m, 1024), _pick(n, 512, LANES)
    return pl.pallas_call(
        _mm_res_kernel,
        grid=(m // tm, n // tn),
        in_specs=[
            pl.BlockSpec((tm, k), lambda i, j: (i, 0)),
            pl.BlockSpec((k, tn), lambda i, j: (0, j)),
            pl.BlockSpec((1, tn), lambda i, j: (0, j)),
            pl.BlockSpec((tm, tn), lambda i, j: (i, j)),
        ],
        out_specs=pl.BlockSpec((tm, tn), lambda i, j: (i, j)),
        out_shape=jax.ShapeDtypeStruct((m, n), F32),
        compiler_params=_params("parallel", "parallel"),
        name="mm_res",
    )(x, w, b, res)


def _head_rms(y, gain, scale):
    out = []
    for c in range(y.shape[1] // LANES):
        yc = y[:, c * LANES:(c + 1) * LANES]
        ms = jnp.mean(yc * yc, axis=-1, keepdims=True)
        yc = yc * lax.rsqrt(ms + EPS) * gain
        out.append(yc * scale if scale != 1.0 else yc)
    return out


def _q_kernel(x_ref, g_ref, w_ref, qn_ref, q_ref, h_scr, *, scale):
    @pl.when(pl.program_id(1) == 0)
    def _():
        h_scr[...] = _rms_bf16(x_ref[...], g_ref[...])

    y = _dot(h_scr[...], w_ref[...])
    for c, yc in enumerate(_head_rms(y, qn_ref[...], scale)):
        q_ref[:, c * LANES:(c + 1) * LANES] = yc.astype(q_ref.dtype)


def _q_proj(x, gain, w, qn, scale):
    m, d = x.shape
    n = w.shape[1]
    tm, tn = _pick(m, 1024), _pick(n, 512, LANES)
    return pl.pallas_call(
        functools.partial(_q_kernel, scale=scale),
        grid=(m // tm, n // tn),
        in_specs=[
            pl.BlockSpec((tm, d), lambda i, j: (i, 0)),
            pl.BlockSpec((1, d), lambda i, j: (0, 0)),
            pl.BlockSpec((d, tn), lambda i, j: (0, j)),
            pl.BlockSpec((1, LANES), lambda i, j: (0, 0)),
        ],
        out_specs=pl.BlockSpec((tm, tn), lambda i, j: (i, j)),
        out_shape=jax.ShapeDtypeStruct((m, n), BF16),
        scratch_shapes=[pltpu.VMEM((tm, d), BF16)],
        compiler_params=_params("parallel", "arbitrary"),
        name="q_proj",
    )(x, gain, w, qn)


def _kv_kernel(x_ref, g_ref, wk_ref, wv_ref, kn_ref, k_ref, v_ref, kb_ref, vb_ref, h_scr):
    @pl.when(pl.program_id(1) == 0)
    def _():
        h_scr[...] = _rms_bf16(x_ref[...], g_ref[...])

    h = h_scr[...]
    k = _dot(h, wk_ref[...])
    for c, kc in enumerate(_head_rms(k, kn_ref[...], 1.0)):
        k_ref[:, c * LANES:(c + 1) * LANES] = kc
        kb_ref[:, c * LANES:(c + 1) * LANES] = kc.astype(BF16)
    v = _dot(h, wv_ref[...])
    v_ref[...] = v
    vb_ref[...] = v.astype(BF16)


def _kv_proj(x, gain, wk, wv, kn):
    m, d = x.shape
    n = wk.shape[1]
    tm, tn = _pick(m, 1024), _pick(n, 512, LANES)
    blk = pl.BlockSpec((tm, tn), lambda i, j: (i, j))
    return pl.pallas_call(
        _kv_kernel,
        grid=(m // tm, n // tn),
        in_specs=[
            pl.BlockSpec((tm, d), lambda i, j: (i, 0)),
            pl.BlockSpec((1, d), lambda i, j: (0, 0)),
            pl.BlockSpec((d, tn), lambda i, j: (0, j)),
            pl.BlockSpec((d, tn), lambda i, j: (0, j)),
            pl.BlockSpec((1, LANES), lambda i, j: (0, 0)),
        ],
        out_specs=[blk, blk, blk, blk],
        out_shape=[jax.ShapeDtypeStruct((m, n), F32), jax.ShapeDtypeStruct((m, n), F32),
                   jax.ShapeDtypeStruct((m, n), BF16), jax.ShapeDtypeStruct((m, n), BF16)],
        scratch_shapes=[pltpu.VMEM((tm, d), BF16)],
        compiler_params=_params("parallel", "arbitrary"),
        name="kv_proj",
    )(x, gain, wk, wv, kn)


def _strict_lower(n):
    row = lax.broadcasted_iota(jnp.int32, (n, n), 0)
    col = lax.broadcasted_iota(jnp.int32, (n, n), 1)
    return col < row


def _sb_block(q, k, v, tri, later, mask):
    z = lax.dot_general(q, k, (((1,), (1,)), ((), ())), preferred_element_type=F32)
    l1p = jnp.log(1.0 + jnp.exp(-jnp.abs(z)))
    log_keep = -(jnp.maximum(z, 0.0) + l1p)
    log_beta = jnp.minimum(z, 0.0) - l1p
    if mask is not None:
        log_keep = jnp.where(mask, log_keep, 0.0)
    in_blk = _dot(log_keep.astype(BF16), tri)
    w = jnp.exp(log_beta + in_blk + later)
    if mask is not None:
        w = jnp.where(mask, w, 0.0)
    o = _dot(w.astype(BF16), v)
    return o, later + jnp.sum(log_keep, axis=-1, keepdims=True)


def _sb_sweep(q_ref, diag_kv, load_kv, tri_diag, tri, n_earlier, acc_scr):
    tq = q_ref.shape[0]
    n_heads = q_ref.shape[1] // LANES
    lanes = [slice(h * LANES, (h + 1) * LANES) for h in range(n_heads)]
    mask = _strict_lower(tq)
    laters = []
    for sl in lanes:
        k, v = diag_kv(sl)
        o, later = _sb_block(q_ref[:, sl], k, v, tri_diag, jnp.zeros((tq, 1), F32), mask)
        acc_scr[:, sl] = o
        laters.append(later)

    def cond(c):
        worst = functools.reduce(jnp.maximum, c[1:])
        return jnp.logical_and(c[0] >= 0, jnp.max(worst) > LOG_ZERO)

    def body(c):
        kb = c[0]
        out = [kb - 1]
        for h, sl in enumerate(lanes):
            k, v = load_kv(kb, sl)
            do, later = _sb_block(q_ref[:, sl], k, v, tri, c[1 + h], None)
            acc_scr[:, sl] += do
            out.append(later)
        return tuple(out)

    lax.while_loop(cond, body, (n_earlier - 1, *laters))


def _attn_self_kernel(q_ref, k_ref, v_ref, o_ref, acc_scr, *, bq):
    i = pl.program_id(2)
    tri = jnp.where(_strict_lower(bq), 1.0, 0.0).astype(BF16)

    def load_kv(kb, sl):
        k0 = pl.multiple_of(kb * bq, bq)
        return k_ref[pl.ds(k0, bq), sl], v_ref[pl.ds(k0, bq), sl]

    _sb_sweep(q_ref, functools.partial(load_kv, i), load_kv, tri, tri, i, acc_scr)
    o_ref[...] = acc_scr[...].astype(o_ref.dtype)


def _attn_self(q, k, v, batch, seq):
    m, hd = q.shape
    hb = _pick(hd, 4 * LANES, LANES)
    bq = _pick(seq, 256, LANES)
    nq = seq // bq
    kv_spec = pl.BlockSpec((seq, hb), lambda b, h, i: (b, h), pipeline_mode=pl.Buffered(1))
    return pl.pallas_call(
        functools.partial(_attn_self_kernel, bq=bq),
        grid=(batch, hd // hb, nq),
        in_specs=[pl.BlockSpec((bq, hb), lambda b, h, i: (b * nq + i, h)), kv_spec, kv_spec],
        out_specs=pl.BlockSpec((bq, hb), lambda b, h, i: (b * nq + i, h)),
        out_shape=jax.ShapeDtypeStruct((m, hd), BF16),
        scratch_shapes=[pltpu.VMEM((bq, hb), F32)],
        compiler_params=_params("parallel", "parallel", "arbitrary"),
        name="attn_self",
    )(q, k, v)


def _attn_cached_kernel(q_ref, kn_ref, vn_ref, ck_ref, cv_ref, o_ref, acc_scr, *, bk):
    tq = q_ref.shape[0]
    past = ck_ref.shape[1]
    tri = jnp.where(_strict_lower(bk), 1.0, 0.0).astype(BF16)

    def diag_kv(sl):
        return kn_ref[:, sl], vn_ref[:, sl]

    def load_kv(kb, sl):
        k0 = pl.multiple_of(kb * bk, bk)
        return ck_ref[0, pl.ds(k0, bk), sl].astype(BF16), cv_ref[0, pl.ds(k0, bk), sl].astype(BF16)

    _sb_sweep(q_ref, diag_kv, load_kv, tri[:tq, :tq], tri, past // bk, acc_scr)
    o_ref[...] = acc_scr[...].astype(o_ref.dtype)


def _attn_cached(q, k_new, v_new, cache_k, cache_v, batch, seq):
    m, hd = q.shape
    hb = _pick(hd, 8 * LANES, LANES)
    past = cache_k.shape[1]
    bk = _pick(past, 256, LANES)
    assert seq <= bk and seq % 16 == 0
    new = pl.BlockSpec((seq, hb), lambda b, h: (b, h))
    old = pl.BlockSpec((1, past, hb), lambda b, h: (b, 0, h))
    return pl.pallas_call(
        functools.partial(_attn_cached_kernel, bk=bk),
        grid=(batch, hd // hb),
        in_specs=[new, new, new, old, old],
        out_specs=new,
        out_shape=jax.ShapeDtypeStruct((m, hd), BF16),
        scratch_shapes=[pltpu.VMEM((seq, hb), F32)],
        compiler_params=_params("parallel", "parallel"),
        name="attn_cached",
    )(q, k_new, v_new, cache_k, cache_v)


def _ffn_kernel(x_ref, g_ref, wg_ref, wu_ref, wd_ref, o_ref, h_scr):
    @pl.when(pl.program_id(1) == 0)
    def _():
        x = x_ref[...]
        h_scr[...] = _rms_bf16(x, g_ref[...])
        o_ref[...] = x

    h = h_scr[...]
    act = jax.nn.silu(_dot(h, wg_ref[...])) * _dot(h, wu_ref[...])
    o_ref[...] += _dot(act.astype(BF16), wd_ref[...])


def _ffn(x, gain, wg, wu, wd):
    m, d = x.shape
    f = wg.shape[1]
    tm, tf = _pick(m, 512), _pick(f, 512, LANES)
    return pl.pallas_call(
        _ffn_kernel,
        grid=(m // tm, f // tf),
        in_specs=[
            pl.BlockSpec((tm, d), lambda i, j: (i, 0)),
            pl.BlockSpec((1, d), lambda i, j: (0, 0)),
            pl.BlockSpec((d, tf), lambda i, j: (0, j)),
            pl.BlockSpec((d, tf), lambda i, j: (0, j)),
            pl.BlockSpec((tf, d), lambda i, j: (j, 0)),
        ],
        out_specs=pl.BlockSpec((tm, d), lambda i, j: (i, 0)),
        out_shape=jax.ShapeDtypeStruct((m, d), F32),
        scratch_shapes=[pltpu.VMEM((tm, d), BF16)],
        compiler_params=_params("parallel", "arbitrary"),
        name="ffn",
    )(x, gain, wg, wu, wd)


def _split_bf16(a):
    hi = a.astype(BF16)
    return hi, (a - hi.astype(F32)).astype(BF16)


def _pack_rows(y, dst_ref):
    tm, d = y.shape
    r = d // (2 * LANES)
    half = d // 2
    for j in range(r):
        hi = pltpu.bitcast(y[:, j * LANES:(j + 1) * LANES].astype(BF16).astype(F32), jnp.uint32)
        lo = pltpu.bitcast(y[:, half + j * LANES:half + (j + 1) * LANES].astype(BF16).astype(F32), jnp.uint32)
        dst_ref[pl.ds(j, tm, stride=r), :] = hi | (lo >> 16)


def _unpack_rows(src_ref, tm, r):
    his, los = [], []
    for j in range(r):
        p = src_ref[pl.ds(j, tm, stride=r), :]
        his.append(pltpu.bitcast(p & jnp.uint32(0xFFFF0000), F32))
        los.append(pltpu.bitcast(p << 16, F32))
    return his, los


def _router_kernel(x_ref, g_ref, wr_ref, hp_ref, route_ref, counts_ref, run_scr, *, n_experts):
    @pl.when(pl.program_id(0) == 0)
    def _():
        run_scr[...] = jnp.zeros_like(run_scr)

    x = x_ref[...]
    tm = x.shape[0]
    ms = jnp.mean(x * x, axis=-1, keepdims=True)
    h = x * lax.rsqrt(ms + EPS) * g_ref[...]
    _pack_rows(h, hp_ref)
    h_hi, h_lo = _split_bf16(h)
    w_hi, w_lo = _split_bf16(wr_ref[...])
    logits = _dot(h_hi, w_hi) + (_dot(h_hi, w_lo) + _dot(h_lo, w_hi))
    lane = lax.broadcasted_iota(jnp.int32, logits.shape, 1).astype(F32)
    neg = -jnp.inf
    l1 = jnp.where(lane < n_experts, logits, neg)
    m1 = jnp.max(l1, axis=-1, keepdims=True)
    i1 = jnp.min(jnp.where(l1 == m1, lane, float(LANES)), axis=-1, keepdims=True)
    l2 = jnp.where(lane == i1, neg, l1)
    m2 = jnp.max(l2, axis=-1, keepdims=True)
    i2 = jnp.min(jnp.where(l2 == m2, lane, float(LANES)), axis=-1, keepdims=True)
    e2 = jnp.exp(m2 - m1)
    den = 1.0 + e2
    hot1 = lane == i1
    hot2 = lane == i2
    both = jnp.where(jnp.logical_or(hot1, hot2), 1.0, 0.0)
    earlier = jnp.where(_strict_lower(tm), 1.0, 0.0).astype(BF16)
    before = run_scr[...] + _dot(earlier, both.astype(BF16))
    rank1 = jnp.sum(jnp.where(hot1, before, 0.0), axis=-1, keepdims=True)
    rank2 = jnp.sum(jnp.where(hot2, before, 0.0), axis=-1, keepdims=True)
    run_scr[...] += jnp.sum(both, axis=0, keepdims=True)
    counts_ref[...] = run_scr[...]
    route = jnp.zeros_like(logits)
    for k, val in enumerate((i1, i2, 1.0 / den, e2 / den, rank1, rank2)):
        route = jnp.where(lane == float(k), val, route)
    route_ref[...] = route


def _router(x, gain, wr):
    m, d = x.shape
    n_experts = wr.shape[1]
    assert n_experts <= LANES and d % (2 * LANES) == 0
    r = d // (2 * LANES)
    wr_p = jnp.pad(wr, ((0, 0), (0, LANES - n_experts)))
    tm = _pick(m, 512)
    return pl.pallas_call(
        functools.partial(_router_kernel, n_experts=n_experts),
        grid=(m // tm,),
        in_specs=[
            pl.BlockSpec((tm, d), lambda i: (i, 0)),
            pl.BlockSpec((1, d), lambda i: (0, 0)),
            pl.BlockSpec((d, LANES), lambda i: (0, 0)),
        ],
        out_specs=[
            pl.BlockSpec((tm * r, LANES), lambda i: (i, 0)),
            pl.BlockSpec((tm, LANES), lambda i: (i, 0)),
            pl.BlockSpec((1, LANES), lambda i: (0, 0)),
        ],
        out_shape=[jax.ShapeDtypeStruct((m * r, LANES), jnp.uint32),
                   jax.ShapeDtypeStruct((m, LANES), F32),
                   jax.ShapeDtypeStruct((1, LANES), F32)],
        scratch_shapes=[pltpu.VMEM((1, LANES), F32)],
        compiler_params=_params("arbitrary"),
        name="router",
    )(x, gain, wr_p)


def _aligned(v, mult):
    return v if isinstance(v, int) else pl.multiple_of(v, mult)


def _moe_ffn_kernel(tile_expert_ref, tile_rows_ref, n_valid_ref, hp_hbm, meta_hbm, gate_ref, wg_ref, wu_ref,
                    wd_ref, y_hbm, xbuf, ybuf, meta_smem, gsem, ssem, msem, *, tm, r, chunk, unroll):
    del tile_expert_ref
    t = pl.program_id(0)
    n_valid = n_valid_ref[0]

    def meta_copy(tile):
        slot = tile % 3
        return pltpu.make_async_copy(meta_hbm.at[pl.ds(_aligned(tile * chunk, chunk), chunk)],
                                     meta_smem.at[pl.ds(_aligned(slot * chunk, chunk), chunk)], msem.at[slot])

    def rows(ref, row):
        return ref.at[pl.ds(_aligned(row * r, r), r), :]

    def gather_copy(tile, row):
        tok = meta_smem[(tile % 3) * chunk + row]
        return pltpu.make_async_copy(rows(hp_hbm, tok), rows(xbuf.at[tile % 2], row), gsem.at[tile % 2])

    def scatter_copy(tile, row):
        dst = meta_smem[(tile % 3) * chunk + tm + row]
        return pltpu.make_async_copy(rows(ybuf.at[tile % 2], row), rows(y_hbm, dst), ssem.at[tile % 2])

    def gather_wait(tile):
        pltpu.make_async_copy(rows(hp_hbm, 0), rows(xbuf.at[tile % 2], 0), gsem.at[tile % 2]).wait()

    def scatter_wait(tile):
        pltpu.make_async_copy(rows(ybuf.at[tile % 2], 0), rows(y_hbm, 0), ssem.at[tile % 2]).wait()

    def for_rows(fn, n=None):
        def body(row, carry):
            fn(row)
            return carry

        if n is None:
            lax.fori_loop(0, tm, body, 0, unroll=unroll)
            return

        def group(g, carry):
            for j in range(unroll):
                fn(g * unroll + j)
            return carry

        lax.fori_loop(0, n // unroll, group, 0)
        lax.fori_loop((n // unroll) * unroll, n, body, 0)

    def valid_rows(tile):
        return tile_rows_ref[tile]

    @pl.when(t == 0)
    def _():
        meta_copy(0).start()

        @pl.when(n_valid > 1)
        def _():
            meta_copy(1).start()

        meta_copy(0).wait()
        for_rows(lambda row: gather_copy(0, row).start())

    @pl.when(t + 2 < n_valid)
    def _():
        meta_copy(t + 2).start()

    @pl.when(t + 1 < n_valid)
    def _():
        meta_copy(t + 1).wait()
        for_rows(lambda row: gather_copy(t + 1, row).start())

    @pl.when(t < n_valid)
    def _():
        for_rows(lambda row: gather_wait(t))

        @pl.when(t >= 2)
        def _():
            for_rows(lambda row: scatter_wait(t - 2), valid_rows(t - 2))

        his, los = _unpack_rows(xbuf.at[t % 2], tm, r)
        x = jnp.concatenate([p.astype(BF16) for p in his + los], axis=1)
        act = jax.nn.silu(_dot(x, wg_ref[0])) * _dot(x, wu_ref[0]) * gate_ref[...]
        _pack_rows(_dot(act.astype(BF16), wd_ref[0]), ybuf.at[t % 2])
        for_rows(lambda row: scatter_copy(t, row).start(), valid_rows(t))

    @pl.when(t == n_valid - 1)
    def _():
        @pl.when(t >= 1)
        def _():
            for_rows(lambda row: scatter_wait(t - 1), valid_rows(t - 1))

        for_rows(lambda row: scatter_wait(t), valid_rows(t))


def _combine_kernel(x_ref, y0_ref, y1_ref, o_ref, *, r):
    tm, d = x_ref.shape
    half = d // 2
    h0, l0 = _unpack_rows(y0_ref, tm, r)
    h1, l1 = _unpack_rows(y1_ref, tm, r)
    for j in range(r):
        a = slice(j * LANES, (j + 1) * LANES)
        b = slice(half + j * LANES, half + (j + 1) * LANES)
        o_ref[:, a] = x_ref[:, a] + (h0[j] + h1[j])
        o_ref[:, b] = x_ref[:, b] + (l0[j] + l1[j])


def _moe(x, gain, wr, wg, wu, wd):
    m, d = x.shape
    n_experts, _, f = wg.shape
    r = d // (2 * LANES)
    hp, route, counts = _router(x, gain, wr)

    tm = _pick(m, min(512, max(256, m // (2 * n_experts))))
    n_tiles = (2 * m) // tm + n_experts
    n_rows = n_tiles * tm
    expert = route[:, 0:2].astype(jnp.int32)
    rank = route[:, 4:6].astype(jnp.int32)
    cnt = counts[0, :n_experts].astype(jnp.int32)
    tiles_e = (cnt + tm - 1) // tm
    tile_end = jnp.cumsum(tiles_e)
    n_valid = tile_end[-1]
    pos = ((tile_end - tiles_e)[expert] * tm + rank).reshape(-1)
    token = jnp.repeat(jnp.arange(m, dtype=jnp.int32), 2)
    slot_row = jnp.tile(jnp.arange(2, dtype=jnp.int32) * m, m) + token
    row_token = jnp.zeros((n_rows,), jnp.int32).at[pos].set(token)
    row_dest = jnp.zeros((n_rows,), jnp.int32).at[pos].set(slot_row)
    row_gate = jnp.zeros((n_rows,), F32).at[pos].set(route[:, 2:4].reshape(-1))
    chunk = -(-2 * tm // SMEM_CHUNK) * SMEM_CHUNK
    meta = jnp.concatenate([row_token.reshape(n_tiles, tm), row_dest.reshape(n_tiles, tm),
                            jnp.zeros((n_tiles, chunk - 2 * tm), jnp.int32)], axis=1).reshape(-1)
    tile_ids = jnp.arange(n_tiles, dtype=jnp.int32)
    tile_expert = jnp.searchsorted(tile_end, jnp.minimum(tile_ids, n_valid - 1), side="right").astype(jnp.int32)
    tile_rows = jnp.clip(cnt[tile_expert] - (tile_ids - (tile_end - tiles_e)[tile_expert]) * tm, 0, tm)
    tile_rows = jnp.where(tile_ids < n_valid, tile_rows, 0).astype(jnp.int32)

    y = pl.pallas_call(
        functools.partial(_moe_ffn_kernel, tm=tm, r=r, chunk=chunk, unroll=8),
        grid_spec=pltpu.PrefetchScalarGridSpec(
            num_scalar_prefetch=3,
            grid=(n_tiles,),
            in_specs=[
                pl.BlockSpec(memory_space=pl.ANY),
                pl.BlockSpec(memory_space=pl.ANY),
                pl.BlockSpec((tm, 1), lambda t, te, tr, nv: (t, 0)),
                pl.BlockSpec((1, d, f), lambda t, te, tr, nv: (te[t], 0, 0)),
                pl.BlockSpec((1, d, f), lambda t, te, tr, nv: (te[t], 0, 0)),
                pl.BlockSpec((1, f, d), lambda t, te, tr, nv: (te[t], 0, 0)),
            ],
            out_specs=pl.BlockSpec(memory_space=pl.ANY),
            scratch_shapes=[
                pltpu.VMEM((2, tm * r, LANES), jnp.uint32),
                pltpu.VMEM((2, tm * r, LANES), jnp.uint32),
                pltpu.SMEM((3 * chunk,), jnp.int32),
                pltpu.SemaphoreType.DMA((2,)),
                pltpu.SemaphoreType.DMA((2,)),
                pltpu.SemaphoreType.DMA((3,)),
            ],
        ),
        out_shape=jax.ShapeDtypeStruct((2 * m * r, LANES), jnp.uint32),
        compiler_params=_params("arbitrary"),
        name="moe_ffn",
    )(tile_expert, tile_rows, n_valid.reshape(1), hp, meta, row_gate.reshape(n_rows, 1), wg, wu, wd)

    tc = _pick(m, 512)
    nb = m // tc
    return pl.pallas_call(
        functools.partial(_combine_kernel, r=r),
        grid=(nb,),
        in_specs=[
            pl.BlockSpec((tc, d), lambda i: (i, 0)),
            pl.BlockSpec((tc * r, LANES), lambda i: (i, 0)),
            pl.BlockSpec((tc * r, LANES), lambda i: (nb + i, 0)),
        ],
        out_specs=pl.BlockSpec((tc, d), lambda i: (i, 0)),
        out_shape=jax.ShapeDtypeStruct((m, d), F32),
        compiler_params=_params("parallel"),
        name="moe_combine",
    )(x, y, y)


def _trunk(x, batch, seq, conv_prev, cache, w):
    d = x.shape[1]
    depth = w["norm_mix"].shape[0]
    n_conv = w["conv_w_dw"].shape[0]
    width = w["conv_w_dw"].shape[1]
    scale = float(LANES) ** -0.5
    states = []
    k_f32 = v_f32 = k_b = v_b = None
    for layer in range(depth):
        if layer == n_conv:
            k_f32, v_f32, k_b, v_b = _kv_proj(x, w["kv_norm"], w["wk"], w["wv"], w["k_norm"])
        gain = w["norm_mix"][layer:layer + 1]
        if layer < n_conv:
            u = _glu_in(x, gain, w["conv_wa"][layer], w["conv_wg"][layer],
                        w["conv_ba"][layer], w["conv_bg"][layer])
            prev = conv_prev[layer]
            u3 = u.reshape(batch, seq, d)
            keep = width - 1
            if seq >= keep:
                states.append(u3[:, seq - keep:])
            else:
                states.append(jnp.concatenate([prev[:, seq:], u3], axis=1))
            prev_p = jnp.pad(prev, ((0, 0), (HALO - keep, 0), (0, 0)))
            c = _conv_ln_swish(u, prev_p, w["conv_w_dw"][layer], w["conv_b_dw"][layer:layer + 1],
                               w["conv_ln_g"][layer:layer + 1], w["conv_ln_b"][layer:layer + 1], batch, seq)
            x = _mm_res(c, w["conv_w_out"][layer], w["conv_b_out"][layer:layer + 1], x)
        else:
            j = layer - n_conv
            q = _q_proj(x, gain, w["w_q"][j], w["q_norm"][j:j + 1], scale)
            if cache is None:
                o = _attn_self(q, k_b, v_b, batch, seq)
            else:
                o = _attn_cached(q, k_b, v_b, cache[0], cache[1], batch, seq)
            x = _mm_res(o, w["w_o"][j], jnp.zeros((1, d), F32), x)
        gain = w["norm_ffn"][layer:layer + 1]
        i = layer // 2
        if layer % 2 == 0:
            x = _ffn(x, gain, w["ffn_w_gate"][i], w["ffn_w_up"][i], w["ffn_w_down"][i])
        else:
            x = _moe(x, gain, w["moe_router"][i], w["moe_w_gate"][i], w["moe_w_up"][i], w["moe_w_down"][i])
    return x, k_f32, v_f32, jnp.stack(states)


def kernel(x_prompt, x_sample, cache_k, cache_v, state_conv, norm_mix, norm_ffn, conv_w_in, conv_b_in, conv_w_dw, conv_b_dw, conv_ln_g, conv_ln_b, conv_w_out, conv_b_out, kv_norm, w_kv, k_norm, w_q, q_norm, w_o, ffn_w_gate, ffn_w_up, ffn_w_down, moe_router, moe_w_gate, moe_w_up, moe_w_down):
    d = x_prompt.shape[-1]
    heads = d // LANES
    assert k_norm.shape[0] == LANES and w_kv.shape[1] == 2 * d
    w = {
        "norm_mix": norm_mix, "norm_ffn": norm_ffn,
        "conv_wa": conv_w_in[:, :, :d].astype(BF16), "conv_wg": conv_w_in[:, :, d:].astype(BF16),
        "conv_ba": conv_b_in[:, None, :d], "conv_bg": conv_b_in[:, None, d:],
        "conv_w_dw": conv_w_dw, "conv_b_dw": conv_b_dw, "conv_ln_g": conv_ln_g, "conv_ln_b": conv_ln_b,
        "conv_w_out": conv_w_out.astype(BF16), "conv_b_out": conv_b_out,
        "kv_norm": kv_norm[None], "wk": w_kv[:, :d].astype(BF16), "wv": w_kv[:, d:].astype(BF16),
        "k_norm": k_norm[None], "w_q": w_q.astype(BF16), "q_norm": q_norm, "w_o": w_o.astype(BF16),
        "ffn_w_gate": ffn_w_gate.astype(BF16), "ffn_w_up": ffn_w_up.astype(BF16),
        "ffn_w_down": ffn_w_down.astype(BF16),
        "moe_router": moe_router, "moe_w_gate": moe_w_gate.astype(BF16),
        "moe_w_up": moe_w_up.astype(BF16), "moe_w_down": moe_w_down.astype(BF16),
    }
    n_conv, width = conv_w_dw.shape[0], conv_w_dw.shape[1]

    bp, sp, _ = x_prompt.shape
    zero_prev = jnp.zeros((n_conv, bp, width - 1, d), x_prompt.dtype)
    y_p, k_p, v_p, conv_p = _trunk(x_prompt.reshape(bp * sp, d), bp, sp, zero_prev, None, w)

    bs, ss, _ = x_sample.shape
    past = cache_k.shape[1]
    cache = (cache_k.reshape(bs, past, d), cache_v.reshape(bs, past, d))
    y_s, k_s, v_s, conv_s = _trunk(x_sample.reshape(bs * ss, d), bs, ss, state_conv, cache, w)

    return (y_p.reshape(bp, sp, d), y_s.reshape(bs, ss, d),
            k_p.reshape(bp, sp, heads, LANES), v_p.reshape(bp, sp, heads, LANES), conv_p,
            k_s.reshape(bs, ss, heads, LANES), v_s.reshape(bs, ss, heads, LANES), conv_s)
```

```python
import functools

import jax
import jax.numpy as jnp
from jax import lax
from jax.experimental import pallas as pl
from jax.experimental.pallas import tpu as pltpu

F32 = jnp.float32
BF16 = jnp.bfloat16

EPS = 1e-6
LANES = 128
HALO = 32
SMEM_CHUNK = 1024
VMEM_LIMIT = 56 * 1024 * 1024
LOG2_E = 1.4426950408889634
LOG_ZERO = -151.0


def _params(*sem):
    return pltpu.CompilerParams(dimension_semantics=sem, vmem_limit_bytes=VMEM_LIMIT)


def _pick(n, pref, mult=8):
    if n <= pref:
        return n
    t = (pref // mult) * mult
    while t > mult and n % t:
        t -= mult
    assert n % t == 0, (n, pref, mult)
    return t


def _rms_bf16(x, g):
    ms = jnp.mean(x * x, axis=-1, keepdims=True)
    return (x * lax.rsqrt(ms + EPS) * g).astype(BF16)


def _dot(a, b):
    return jnp.dot(a, b, preferred_element_type=F32)


def _glu_in_kernel(x_ref, g_ref, wa_ref, wg_ref, ba_ref, bg_ref, u_ref, h_scr):
    @pl.when(pl.program_id(1) == 0)
    def _():
        h_scr[...] = _rms_bf16(x_ref[...], g_ref[...])

    h = h_scr[...]
    a = _dot(h, wa_ref[...]) + ba_ref[...]
    g = _dot(h, wg_ref[...]) + bg_ref[...]
    u_ref[...] = a * jax.nn.sigmoid(g)


def _glu_in(x, gain, wa, wg, ba, bg):
    m, d = x.shape
    n = wa.shape[1]
    tm, tn = _pick(m, 1024), _pick(n, 512, LANES)
    return pl.pallas_call(
        _glu_in_kernel,
        grid=(m // tm, n // tn),
        in_specs=[
            pl.BlockSpec((tm, d), lambda i, j: (i, 0)),
            pl.BlockSpec((1, d), lambda i, j: (0, 0)),
            pl.BlockSpec((d, tn), lambda i, j: (0, j)),
            pl.BlockSpec((d, tn), lambda i, j: (0, j)),
            pl.BlockSpec((1, tn), lambda i, j: (0, j)),
            pl.BlockSpec((1, tn), lambda i, j: (0, j)),
        ],
        out_specs=pl.BlockSpec((tm, tn), lambda i, j: (i, j)),
        out_shape=jax.ShapeDtypeStruct((m, n), F32),
        scratch_shapes=[pltpu.VMEM((tm, d), BF16)],
        compiler_params=_params("parallel", "arbitrary"),
        name="glu_in",
    )(x, gain, wa, wg, ba, bg)


def _conv_kernel(u_ref, halo_ref, prev_ref, wdw_ref, bdw_ref, lng_ref, lnb_ref, c_ref,
                 ext_scr, sh_scr, acc_scr, *, tc, width, rows_ln):
    d = u_ref.shape[1]
    i = pl.program_id(1)

    @pl.when(i == 0)
    def _():
        ext_scr[0:HALO, :] = prev_ref[0]

    @pl.when(i > 0)
    def _():
        ext_scr[0:HALO, :] = halo_ref[...]

    ext_scr[HALO:HALO + tc, :] = u_ref[...]

    off = HALO - (width - 1)
    n_shift = 8
    rc = min(tc, 64)

    def lane_body(c, carry):
        l0 = pl.multiple_of(c * LANES, LANES)
        for r in range(n_shift):
            taps_r = [w for w in range(width) if w % n_shift == r]
            n_rows = tc + (max(taps_r) - r)
            sh_scr[r, 0:n_rows, :] = ext_scr[off + r:off + r + n_rows, pl.ds(l0, LANES)]
        for r0 in range(0, tc, rc):
            acc = jnp.broadcast_to(bdw_ref[:, pl.ds(l0, LANES)], (rc, LANES))
            for w in range(width):
                r, q = w % n_shift, w // n_shift
                win = sh_scr[r, r0 + 8 * q:r0 + 8 * q + rc, :]
                acc = acc + win * wdw_ref[w:w + 1, pl.ds(l0, LANES)]
            acc_scr[r0:r0 + rc, pl.ds(l0, LANES)] = acc
        return carry

    lax.fori_loop(0, d // LANES, lane_body, 0)

    def ln_body(r, carry):
        r0 = pl.multiple_of(r * rows_ln, rows_ln)
        c = acc_scr[pl.ds(r0, rows_ln), :]
        mu = jnp.mean(c, axis=-1, keepdims=True)
        xc = c - mu
        var = jnp.mean(xc * xc, axis=-1, keepdims=True)
        y = xc * lax.rsqrt(var + EPS) * lng_ref[...] + lnb_ref[...]
        c_ref[pl.ds(r0, rows_ln), :] = (y * jax.nn.sigmoid(y)).astype(c_ref.dtype)
        return carry

    lax.fori_loop(0, tc // rows_ln, ln_body, 0, unroll=min(4, tc // rows_ln))


def _conv_ln_swish(u, prev, wdw, bdw, lng, lnb, batch, seq):
    m, d = u.shape
    width = wdw.shape[0]
    assert width - 1 <= HALO and seq % HALO == 0
    tc = _pick(seq, 256, HALO)
    nt = seq // tc
    hb = tc // HALO
    wdw_p = jnp.pad(wdw, ((0, (-width) % 8), (0, 0)))
    kern = functools.partial(_conv_kernel, tc=tc, width=width, rows_ln=16)
    return pl.pallas_call(
        kern,
        grid=(batch, nt),
        in_specs=[
            pl.BlockSpec((tc, d), lambda b, i: (b * nt + i, 0)),
            pl.BlockSpec((HALO, d), lambda b, i: (jnp.maximum((b * nt + i) * hb - 1, 0), 0)),
            pl.BlockSpec((1, HALO, d), lambda b, i: (b, 0, 0)),
            pl.BlockSpec(wdw_p.shape, lambda b, i: (0, 0)),
            pl.BlockSpec((1, d), lambda b, i: (0, 0)),
            pl.BlockSpec((1, d), lambda b, i: (0, 0)),
            pl.BlockSpec((1, d), lambda b, i: (0, 0)),
        ],
        out_specs=pl.BlockSpec((tc, d), lambda b, i: (b * nt + i, 0)),
        out_shape=jax.ShapeDtypeStruct((m, d), BF16),
        scratch_shapes=[
            pltpu.VMEM((HALO + tc, d), F32),
            pltpu.VMEM((8, tc + 24, LANES), F32),
            pltpu.VMEM((tc, d), F32),
        ],
        compiler_params=_params("parallel", "arbitrary"),
        name="conv_ln_swish",
    )(u, u, prev, wdw_p, bdw, lng, lnb)


def _mm_res_kernel(x_ref, w_ref, b_ref, r_ref, o_ref):
    o_ref[...] = r_ref[...] + b_ref[...] + _dot(x_ref[...], w_ref[...])


def _mm_res(x, w, b, res):
    m, k = x.shape
    n = w.shape[1]
    tm, tn = _pick(m, 1024), _pick(n, 1024, LANES)
    return pl.pallas_call(
        _mm_res_kernel,
        grid=(m // tm, n // tn),
        in_specs=[
            pl.BlockSpec((tm, k), lambda i, j: (i, 0)),
            pl.BlockSpec((k, tn), lambda i, j: (0, j)),
            pl.BlockSpec((1, tn), lambda i, j: (0, j)),
            pl.BlockSpec((tm, tn), lambda i, j: (i, j)),
        ],
        out_specs=pl.BlockSpec((tm, tn), lambda i, j: (i, j)),
        out_shape=jax.ShapeDtypeStruct((m, n), F32),
        compiler_params=_params("parallel", "parallel"),
        name="mm_res",
    )(x, w, b, res)


def _head_rms(y, gain, scale):
    out = []
    for c in range(y.shape[1] // LANES):
        yc = y[:, c * LANES:(c + 1) * LANES]
        ms = jnp.mean(yc * yc, axis=-1, keepdims=True)
        yc = yc * lax.rsqrt(ms + EPS) * gain
        out.append(yc * scale if scale != 1.0 else yc)
    return out


def _q_kernel(x_ref, g_ref, w_ref, qn_ref, q_ref, h_scr, *, scale):
    @pl.when(pl.program_id(1) == 0)
    def _():
        h_scr[...] = _rms_bf16(x_ref[...], g_ref[...])

    y = _dot(h_scr[...], w_ref[...])
    for c, yc in enumerate(_head_rms(y, qn_ref[...], scale)):
        q_ref[:, c * LANES:(c + 1) * LANES] = yc.astype(q_ref.dtype)


def _q_proj(x, gain, w, qn, scale):
    m, d = x.shape
    n = w.shape[1]
    tm, tn = _pick(m, 1024), _pick(n, 1024, LANES)
    return pl.pallas_call(
        functools.partial(_q_kernel, scale=scale),
        grid=(m // tm, n // tn),
        in_specs=[
            pl.BlockSpec((tm, d), lambda i, j: (i, 0)),
            pl.BlockSpec((1, d), lambda i, j: (0, 0)),
            pl.BlockSpec((d, tn), lambda i, j: (0, j)),
            pl.BlockSpec((1, LANES), lambda i, j: (0, 0)),
        ],
        out_specs=pl.BlockSpec((tm, tn), lambda i, j: (i, j)),
        out_shape=jax.ShapeDtypeStruct((m, n), BF16),
        scratch_shapes=[pltpu.VMEM((tm, d), BF16)],
        compiler_params=_params("parallel", "arbitrary"),
        name="q_proj",
    )(x, gain, w, qn)


def _kv_kernel(x_ref, g_ref, wk_ref, wv_ref, kn_ref, k_ref, v_ref, kb_ref, vb_ref, h_scr):
    @pl.when(pl.program_id(1) == 0)
    def _():
        h_scr[...] = _rms_bf16(x_ref[...], g_ref[...])

    h = h_scr[...]
    k = _dot(h, wk_ref[...])
    for c, kc in enumerate(_head_rms(k, kn_ref[...], 1.0)):
        k_ref[:, c * LANES:(c + 1) * LANES] = kc
        kb_ref[:, c * LANES:(c + 1) * LANES] = kc.astype(BF16)
    v = _dot(h, wv_ref[...])
    v_ref[...] = v
    vb_ref[...] = v.astype(BF16)


def _kv_proj(x, gain, wk, wv, kn):
    m, d = x.shape
    n = wk.shape[1]
    tm, tn = _pick(m, 1024), _pick(n, 512, LANES)
    blk = pl.BlockSpec((tm, tn), lambda i, j: (i, j))
    return pl.pallas_call(
        _kv_kernel,
        grid=(m // tm, n // tn),
        in_specs=[
            pl.BlockSpec((tm, d), lambda i, j: (i, 0)),
            pl.BlockSpec((1, d), lambda i, j: (0, 0)),
            pl.BlockSpec((d, tn), lambda i, j: (0, j)),
            pl.BlockSpec((d, tn), lambda i, j: (0, j)),
            pl.BlockSpec((1, LANES), lambda i, j: (0, 0)),
        ],
        out_specs=[blk, blk, blk, blk],
        out_shape=[jax.ShapeDtypeStruct((m, n), F32), jax.ShapeDtypeStruct((m, n), F32),
                   jax.ShapeDtypeStruct((m, n), BF16), jax.ShapeDtypeStruct((m, n), BF16)],
        scratch_shapes=[pltpu.VMEM((tm, d), BF16)],
        compiler_params=_params("parallel", "arbitrary"),
        name="kv_proj",
    )(x, gain, wk, wv, kn)


def _strict_lower(n):
    row = lax.broadcasted_iota(jnp.int32, (n, n), 0)
    col = lax.broadcasted_iota(jnp.int32, (n, n), 1)
    return col < row


def _sb_block(q, k, v, tri, later, mask):
    z = lax.dot_general(q, k, (((1,), (1,)), ((), ())), preferred_element_type=F32)
    drop = jnp.maximum(z, 0.0) + LOG2_E * jnp.log(1.0 + jnp.exp2(-jnp.abs(z)))
    if mask is not None:
        drop = jnp.where(mask, drop, 0.0)
    in_blk = _dot(drop.astype(BF16), tri)
    w = jnp.exp2((z - drop) - in_blk + later)
    if mask is not None:
        w = jnp.where(mask, w, 0.0)
    o = _dot(w.astype(BF16), v)
    return o, later - jnp.sum(drop, axis=-1, keepdims=True)


def _sb_sweep(chains, tri_diag, tri, q_ref, acc_scr):
    tq = chains[0][0].stop - chains[0][0].start
    mask = _strict_lower(tq)
    laters = []
    for rows, lanes, diag_kv, _, _ in chains:
        k, v = diag_kv()
        o, later = _sb_block(q_ref[rows, lanes], k, v, tri_diag, jnp.zeros((tq, 1), F32), mask)
        acc_scr[rows, lanes] = o
        laters.append(later)

    def live(j, chain, later):
        n = chain[4]
        return later if isinstance(n, int) else jnp.where(j < n, later, 2 * LOG_ZERO)

    n_max = functools.reduce(jnp.maximum, [c[4] for c in chains])

    def cond(c):
        j = c[0]
        worst = functools.reduce(jnp.maximum, [live(j, ch, lt) for ch, lt in zip(chains, c[1:])])
        return jnp.logical_and(j < n_max, jnp.max(worst) > LOG_ZERO)

    def body(c):
        j = c[0]
        out = [j + 1]
        for chain, later in zip(chains, c[1:]):
            rows, lanes, _, earlier_kv, n = chain
            k, v = earlier_kv(j)
            do, later = _sb_block(q_ref[rows, lanes], k, v, tri, live(j, chain, later), None)
            acc_scr[rows, lanes] += do
            out.append(later)
        return tuple(out)

    lax.while_loop(cond, body, (jnp.int32(0), *laters))


def _attn_self_kernel(q_ref, k_ref, v_ref, o_ref, acc_scr, *, bk, n_sub):
    i = pl.program_id(2)
    tri = jnp.where(_strict_lower(bk), 1.0, 0.0).astype(BF16)

    def load_kv(lanes, kb):
        k0 = pl.multiple_of(kb * bk, bk)
        return k_ref[pl.ds(k0, bk), lanes], v_ref[pl.ds(k0, bk), lanes]

    chains = []
    for s in range(n_sub):
        qb = i * n_sub + s
        for h in range(q_ref.shape[1] // LANES):
            lanes = slice(h * LANES, (h + 1) * LANES)
            chains.append((slice(s * bk, (s + 1) * bk), lanes,
                           functools.partial(load_kv, lanes, qb),
                           lambda j, lanes=lanes, qb=qb: load_kv(lanes, jnp.maximum(qb - 1 - j, 0)),
                           qb))
    _sb_sweep(chains, tri, tri, q_ref, acc_scr)
    o_ref[...] = acc_scr[...].astype(o_ref.dtype)


def _attn_self(q, k, v, batch, seq):
    m, hd = q.shape
    hb = _pick(hd, 4 * LANES, LANES)
    bk = _pick(seq, 256, LANES)
    n_sub = 2 if seq % (2 * bk) == 0 else 1
    bq = n_sub * bk
    nq = seq // bq
    kv_spec = pl.BlockSpec((seq, hb), lambda b, h, i: (b, h), pipeline_mode=pl.Buffered(1))
    return pl.pallas_call(
        functools.partial(_attn_self_kernel, bk=bk, n_sub=n_sub),
        grid=(batch, hd // hb, nq),
        in_specs=[pl.BlockSpec((bq, hb), lambda b, h, i: (b * nq + i, h)), kv_spec, kv_spec],
        out_specs=pl.BlockSpec((bq, hb), lambda b, h, i: (b * nq + i, h)),
        out_shape=jax.ShapeDtypeStruct((m, hd), BF16),
        scratch_shapes=[pltpu.VMEM((bq, hb), F32)],
        compiler_params=_params("parallel", "parallel", "arbitrary"),
        name="attn_self",
    )(q, k, v)


def _attn_cached_kernel(q_ref, kn_ref, vn_ref, ck_ref, cv_ref, o_ref, acc_scr, *, bk):
    tq = q_ref.shape[0]
    past = ck_ref.shape[1]
    tri = jnp.where(_strict_lower(bk), 1.0, 0.0).astype(BF16)

    n_old = past // bk

    def diag_kv(lanes):
        return kn_ref[:, lanes], vn_ref[:, lanes]

    def earlier_kv(lanes, j):
        k0 = pl.multiple_of((n_old - 1 - j) * bk, bk)
        return ck_ref[0, pl.ds(k0, bk), lanes].astype(BF16), cv_ref[0, pl.ds(k0, bk), lanes].astype(BF16)

    chains = []
    for h in range(q_ref.shape[1] // LANES):
        lanes = slice(h * LANES, (h + 1) * LANES)
        chains.append((slice(0, tq), lanes, functools.partial(diag_kv, lanes),
                       functools.partial(earlier_kv, lanes), n_old))
    _sb_sweep(chains, tri[:tq, :tq], tri, q_ref, acc_scr)
    o_ref[...] = acc_scr[...].astype(o_ref.dtype)


def _attn_cached(q, k_new, v_new, cache_k, cache_v, batch, seq):
    m, hd = q.shape
    hb = _pick(hd, 8 * LANES, LANES)
    past = cache_k.shape[1]
    bk = _pick(past, 256, LANES)
    assert seq <= bk and seq % 16 == 0
    new = pl.BlockSpec((seq, hb), lambda b, h: (b, h))
    old = pl.BlockSpec((1, past, hb), lambda b, h: (b, 0, h))
    return pl.pallas_call(
        functools.partial(_attn_cached_kernel, bk=bk),
        grid=(batch, hd // hb),
        in_specs=[new, new, new, old, old],
        out_specs=new,
        out_shape=jax.ShapeDtypeStruct((m, hd), BF16),
        scratch_shapes=[pltpu.VMEM((seq, hb), F32)],
        compiler_params=_params("parallel", "parallel"),
        name="attn_cached",
    )(q, k_new, v_new, cache_k, cache_v)


def _ffn_kernel(x_ref, g_ref, wg_ref, wu_ref, wd_ref, o_ref, h_scr):
    @pl.when(pl.program_id(1) == 0)
    def _():
        x = x_ref[...]
        h_scr[...] = _rms_bf16(x, g_ref[...])
        o_ref[...] = x

    h = h_scr[...]
    act = jax.nn.silu(_dot(h, wg_ref[...])) * _dot(h, wu_ref[...])
    o_ref[...] += _dot(act.astype(BF16), wd_ref[...])


def _ffn(x, gain, wg, wu, wd):
    m, d = x.shape
    f = wg.shape[1]
    tm, tf = _pick(m, 512), _pick(f, 512, LANES)
    return pl.pallas_call(
        _ffn_kernel,
        grid=(m // tm, f // tf),
        in_specs=[
            pl.BlockSpec((tm, d), lambda i, j: (i, 0)),
            pl.BlockSpec((1, d), lambda i, j: (0, 0)),
            pl.BlockSpec((d, tf), lambda i, j: (0, j)),
            pl.BlockSpec((d, tf), lambda i, j: (0, j)),
            pl.BlockSpec((tf, d), lambda i, j: (j, 0)),
        ],
        out_specs=pl.BlockSpec((tm, d), lambda i, j: (i, 0)),
        out_shape=jax.ShapeDtypeStruct((m, d), F32),
        scratch_shapes=[pltpu.VMEM((tm, d), BF16)],
        compiler_params=_params("parallel", "arbitrary"),
        name="ffn",
    )(x, gain, wg, wu, wd)


def _split_bf16(a):
    hi = a.astype(BF16)
    return hi, (a - hi.astype(F32)).astype(BF16)


def _pack_rows(y, dst_ref):
    tm, d = y.shape
    r = d // (2 * LANES)
    half = d // 2
    for j in range(r):
        hi = pltpu.bitcast(y[:, j * LANES:(j + 1) * LANES].astype(BF16).astype(F32), jnp.uint32)
        lo = pltpu.bitcast(y[:, half + j * LANES:half + (j + 1) * LANES].astype(BF16).astype(F32), jnp.uint32)
        dst_ref[pl.ds(j, tm, stride=r), :] = hi | (lo >> 16)


def _unpack_rows(src_ref, tm, r):
    his, los = [], []
    for j in range(r):
        p = src_ref[pl.ds(j, tm, stride=r), :]
        his.append(pltpu.bitcast(p & jnp.uint32(0xFFFF0000), F32))
        los.append(pltpu.bitcast(p << 16, F32))
    return his, los


def _router_kernel(x_ref, g_ref, wr_ref, hp_ref, route_ref, counts_ref, run_scr, *, n_experts):
    @pl.when(pl.program_id(0) == 0)
    def _():
        run_scr[...] = jnp.zeros_like(run_scr)

    x = x_ref[...]
    tm = x.shape[0]
    ms = jnp.mean(x * x, axis=-1, keepdims=True)
    h = x * lax.rsqrt(ms + EPS) * g_ref[...]
    _pack_rows(h, hp_ref)
    h_hi, h_lo = _split_bf16(h)
    w_hi, w_lo = _split_bf16(wr_ref[...])
    logits = _dot(h_hi, w_hi) + (_dot(h_hi, w_lo) + _dot(h_lo, w_hi))
    lane = lax.broadcasted_iota(jnp.int32, logits.shape, 1).astype(F32)
    neg = -jnp.inf
    l1 = jnp.where(lane < n_experts, logits, neg)
    m1 = jnp.max(l1, axis=-1, keepdims=True)
    i1 = jnp.min(jnp.where(l1 == m1, lane, float(LANES)), axis=-1, keepdims=True)
    l2 = jnp.where(lane == i1, neg, l1)
    m2 = jnp.max(l2, axis=-1, keepdims=True)
    i2 = jnp.min(jnp.where(l2 == m2, lane, float(LANES)), axis=-1, keepdims=True)
    e2 = jnp.exp(m2 - m1)
    den = 1.0 + e2
    hot1 = lane == i1
    hot2 = lane == i2
    both = jnp.where(jnp.logical_or(hot1, hot2), 1.0, 0.0)
    earlier = jnp.where(_strict_lower(tm), 1.0, 0.0).astype(BF16)
    before = run_scr[...] + _dot(earlier, both.astype(BF16))
    rank1 = jnp.sum(jnp.where(hot1, before, 0.0), axis=-1, keepdims=True)
    rank2 = jnp.sum(jnp.where(hot2, before, 0.0), axis=-1, keepdims=True)
    run_scr[...] += jnp.sum(both, axis=0, keepdims=True)
    counts_ref[...] = run_scr[...]
    route = jnp.zeros_like(logits)
    for k, val in enumerate((i1, i2, 1.0 / den, e2 / den, rank1, rank2)):
        route = jnp.where(lane == float(k), val, route)
    route_ref[...] = route


def _router(x, gain, wr):
    m, d = x.shape
    n_experts = wr.shape[1]
    assert n_experts <= LANES and d % (2 * LANES) == 0
    r = d // (2 * LANES)
    wr_p = jnp.pad(wr, ((0, 0), (0, LANES - n_experts)))
    tm = _pick(m, 512)
    return pl.pallas_call(
        functools.partial(_router_kernel, n_experts=n_experts),
        grid=(m // tm,),
        in_specs=[
            pl.BlockSpec((tm, d), lambda i: (i, 0)),
            pl.BlockSpec((1, d), lambda i: (0, 0)),
            pl.BlockSpec((d, LANES), lambda i: (0, 0)),
        ],
        out_specs=[
            pl.BlockSpec((tm * r, LANES), lambda i: (i, 0)),
            pl.BlockSpec((tm, LANES), lambda i: (i, 0)),
            pl.BlockSpec((1, LANES), lambda i: (0, 0)),
        ],
        out_shape=[jax.ShapeDtypeStruct((m * r, LANES), jnp.uint32),
                   jax.ShapeDtypeStruct((m, LANES), F32),
                   jax.ShapeDtypeStruct((1, LANES), F32)],
        scratch_shapes=[pltpu.VMEM((1, LANES), F32)],
        compiler_params=_params("arbitrary"),
        name="router",
    )(x, gain, wr_p)


def _aligned(v, mult):
    return v if isinstance(v, int) else pl.multiple_of(v, mult)


def _moe_ffn_kernel(tile_expert_ref, tile_rows_ref, n_valid_ref, hp_hbm, meta_hbm, wg_ref, wu_ref, wd_ref,
                    y_hbm, xbuf, ybuf, meta_smem, gsem, ssem, msem, *, tm, r, chunk, unroll, n_tokens):
    del tile_expert_ref
    t = pl.program_id(0)
    n_valid = n_valid_ref[0]

    def meta_copy(tile):
        slot = tile % 3
        return pltpu.make_async_copy(meta_hbm.at[pl.ds(_aligned(tile * chunk, chunk), chunk)],
                                     meta_smem.at[pl.ds(_aligned(slot * chunk, chunk), chunk)], msem.at[slot])

    def rows(ref, row, n=1):
        return ref.at[pl.ds(_aligned(row * r, r), n * r), :]

    def gather_copy(tile, row):
        tok = meta_smem[(tile % 3) * chunk + row] >> 1
        return pltpu.make_async_copy(rows(hp_hbm, tok), rows(xbuf.at[tile % 2], row), gsem.at[tile % 2])

    def scatter_copy(tile, row):
        a = meta_smem[(tile % 3) * chunk + row]
        dst = (a & 1) * n_tokens + (a >> 1)
        return pltpu.make_async_copy(rows(ybuf.at[tile % 2], row), rows(y_hbm, dst), ssem.at[tile % 2])

    def gather_wait(tile, n):
        pltpu.make_async_copy(rows(hp_hbm, 0, n), rows(xbuf.at[tile % 2], 0, n), gsem.at[tile % 2]).wait()

    def scatter_wait(tile, n):
        pltpu.make_async_copy(rows(ybuf.at[tile % 2], 0, n), rows(y_hbm, 0, n), ssem.at[tile % 2]).wait()

    def for_rows(fn, n=None):
        def body(row, carry):
            fn(row)
            return carry

        if n is None:
            lax.fori_loop(0, tm, body, 0, unroll=unroll)
            return

        def group(g, carry):
            for j in range(unroll):
                fn(g * unroll + j)
            return carry

        lax.fori_loop(0, n // unroll, group, 0)
        lax.fori_loop((n // unroll) * unroll, n, body, 0)

    def wait_rows(wait, n=None):
        def many(g, carry):
            wait(unroll)
            return carry

        def one(g, carry):
            wait(1)
            return carry

        if n is None:
            lax.fori_loop(0, tm // unroll, many, 0)
            return
        lax.fori_loop(0, n // unroll, many, 0)
        lax.fori_loop(0, n % unroll, one, 0)

    def valid_rows(tile):
        return tile_rows_ref[tile]

    @pl.when(t == 0)
    def _():
        meta_copy(0).start()

        @pl.when(n_valid > 1)
        def _():
            meta_copy(1).start()

        meta_copy(0).wait()
        for_rows(lambda row: gather_copy(0, row).start())

    @pl.when(t + 2 < n_valid)
    def _():
        meta_copy(t + 2).start()

    @pl.when(t + 1 < n_valid)
    def _():
        meta_copy(t + 1).wait()
        for_rows(lambda row: gather_copy(t + 1, row).start())

    @pl.when(t < n_valid)
    def _():
        wait_rows(functools.partial(gather_wait, t))

        @pl.when(t >= 2)
        def _():
            wait_rows(functools.partial(scatter_wait, t - 2), valid_rows(t - 2))

        his, los = _unpack_rows(xbuf.at[t % 2], tm, r)
        x = jnp.concatenate([p.astype(BF16) for p in his + los], axis=1)
        act = jax.nn.silu(_dot(x, wg_ref[0])) * _dot(x, wu_ref[0])
        _pack_rows(_dot(act.astype(BF16), wd_ref[0]), ybuf.at[t % 2])
        for_rows(lambda row: scatter_copy(t, row).start(), valid_rows(t))

    @pl.when(t == n_valid - 1)
    def _():
        @pl.when(t >= 1)
        def _():
            wait_rows(functools.partial(scatter_wait, t - 1), valid_rows(t - 1))

        wait_rows(functools.partial(scatter_wait, t), valid_rows(t))


def _combine_kernel(x_ref, route_ref, y0_ref, y1_ref, o_ref, *, r):
    tm, d = x_ref.shape
    half = d // 2
    g0 = route_ref[:, 2:3]
    g1 = route_ref[:, 3:4]
    h0, l0 = _unpack_rows(y0_ref, tm, r)
    h1, l1 = _unpack_rows(y1_ref, tm, r)
    for j in range(r):
        a = slice(j * LANES, (j + 1) * LANES)
        b = slice(half + j * LANES, half + (j + 1) * LANES)
        o_ref[:, a] = x_ref[:, a] + (g0 * h0[j] + g1 * h1[j])
        o_ref[:, b] = x_ref[:, b] + (g0 * l0[j] + g1 * l1[j])


def _moe(x, gain, wr, wg, wu, wd):
    m, d = x.shape
    n_experts, _, f = wg.shape
    r = d // (2 * LANES)
    hp, route, counts = _router(x, gain, wr)

    tm = _pick(m, min(512, max(256, m // (2 * n_experts))))
    n_tiles = (2 * m) // tm + n_experts
    n_rows = n_tiles * tm
    expert = route[:, 0:2].astype(jnp.int32)
    rank = route[:, 4:6].astype(jnp.int32)
    cnt = counts[0, :n_experts].astype(jnp.int32)
    tiles_e = (cnt + tm - 1) // tm
    tile_end = jnp.cumsum(tiles_e)
    n_valid = tile_end[-1]
    tile_start = tile_end - tiles_e
    hot = expert[..., None] == jnp.arange(n_experts, dtype=jnp.int32)
    pos = (jnp.sum(jnp.where(hot, tile_start, 0), axis=-1) * tm + rank).reshape(-1)
    row_assign = jnp.zeros((n_rows,), jnp.int32).at[pos].set(jnp.arange(2 * m, dtype=jnp.int32))
    chunk = -(-tm // SMEM_CHUNK) * SMEM_CHUNK
    meta = jnp.pad(row_assign.reshape(n_tiles, tm), ((0, 0), (0, chunk - tm))).reshape(-1)
    tile_ids = jnp.arange(n_tiles, dtype=jnp.int32)
    tile_expert = jnp.searchsorted(tile_end, jnp.minimum(tile_ids, n_valid - 1), side="right").astype(jnp.int32)
    tile_rows = jnp.clip(cnt[tile_expert] - (tile_ids - tile_start[tile_expert]) * tm, 0, tm)
    tile_rows = jnp.where(tile_ids < n_valid, tile_rows, 0).astype(jnp.int32)

    y = pl.pallas_call(
        functools.partial(_moe_ffn_kernel, tm=tm, r=r, chunk=chunk, unroll=8, n_tokens=m),
        grid_spec=pltpu.PrefetchScalarGridSpec(
            num_scalar_prefetch=3,
            grid=(n_tiles,),
            in_specs=[
                pl.BlockSpec(memory_space=pl.ANY),
                pl.BlockSpec(memory_space=pl.ANY),
                pl.BlockSpec((1, d, f), lambda t, te, tr, nv: (te[t], 0, 0)),
                pl.BlockSpec((1, d, f), lambda t, te, tr, nv: (te[t], 0, 0)),
                pl.BlockSpec((1, f, d), lambda t, te, tr, nv: (te[t], 0, 0)),
            ],
            out_specs=pl.BlockSpec(memory_space=pl.ANY),
            scratch_shapes=[
                pltpu.VMEM((2, tm * r, LANES), jnp.uint32),
                pltpu.VMEM((2, tm * r, LANES), jnp.uint32),
                pltpu.SMEM((3 * chunk,), jnp.int32),
                pltpu.SemaphoreType.DMA((2,)),
                pltpu.SemaphoreType.DMA((2,)),
                pltpu.SemaphoreType.DMA((3,)),
            ],
        ),
        out_shape=jax.ShapeDtypeStruct((2 * m * r, LANES), jnp.uint32),
        compiler_params=_params("arbitrary"),
        name="moe_ffn",
    )(tile_expert, tile_rows, n_valid.reshape(1), hp, meta, wg, wu, wd)

    tc = _pick(m, 512)
    nb = m // tc
    return pl.pallas_call(
        functools.partial(_combine_kernel, r=r),
        grid=(nb,),
        in_specs=[
            pl.BlockSpec((tc, d), lambda i: (i, 0)),
            pl.BlockSpec((tc, LANES), lambda i: (i, 0)),
            pl.BlockSpec((tc * r, LANES), lambda i: (i, 0)),
            pl.BlockSpec((tc * r, LANES), lambda i: (nb + i, 0)),
        ],
        out_specs=pl.BlockSpec((tc, d), lambda i: (i, 0)),
        out_shape=jax.ShapeDtypeStruct((m, d), F32),
        compiler_params=_params("parallel"),
        name="moe_combine",
    )(x, route, y, y)


def _trunk(x, batch, seq, conv_prev, cache, w):
    d = x.shape[1]
    depth = w["norm_mix"].shape[0]
    n_conv = w["conv_w_dw"].shape[0]
    width = w["conv_w_dw"].shape[1]
    scale = LOG2_E * float(LANES) ** -0.5
    states = []
    k_f32 = v_f32 = k_b = v_b = None
    for layer in range(depth):
        if layer == n_conv:
            k_f32, v_f32, k_b, v_b = _kv_proj(x, w["kv_norm"], w["wk"], w["wv"], w["k_norm"])
        gain = w["norm_mix"][layer:layer + 1]
        if layer < n_conv:
            u = _glu_in(x, gain, w["conv_wa"][layer], w["conv_wg"][layer],
                        w["conv_ba"][layer], w["conv_bg"][layer])
            prev = conv_prev[layer]
            u3 = u.reshape(batch, seq, d)
            keep = width - 1
            if seq >= keep:
                states.append(u3[:, seq - keep:])
            else:
                states.append(jnp.concatenate([prev[:, seq:], u3], axis=1))
            prev_p = jnp.pad(prev, ((0, 0), (HALO - keep, 0), (0, 0)))
            c = _conv_ln_swish(u, prev_p, w["conv_w_dw"][layer], w["conv_b_dw"][layer:layer + 1],
                               w["conv_ln_g"][layer:layer + 1], w["conv_ln_b"][layer:layer + 1], batch, seq)
            x = _mm_res(c, w["conv_w_out"][layer], w["conv_b_out"][layer:layer + 1], x)
        else:
            j = layer - n_conv
            q = _q_proj(x, gain, w["w_q"][j], w["q_norm"][j:j + 1], scale)
            if cache is None:
                o = _attn_self(q, k_b, v_b, batch, seq)
            else:
                o = _attn_cached(q, k_b, v_b, cache[0], cache[1], batch, seq)
            x = _mm_res(o, w["w_o"][j], jnp.zeros((1, d), F32), x)
        gain = w["norm_ffn"][layer:layer + 1]
        i = layer // 2
        if layer % 2 == 0:
            x = _ffn(x, gain, w["ffn_w_gate"][i], w["ffn_w_up"][i], w["ffn_w_down"][i])
        else:
            x = _moe(x, gain, w["moe_router"][i], w["moe_w_gate"][i], w["moe_w_up"][i], w["moe_w_down"][i])
    return x, k_f32, v_f32, jnp.stack(states)


def kernel(x_prompt, x_sample, cache_k, cache_v, state_conv, norm_mix, norm_ffn, conv_w_in, conv_b_in, conv_w_dw, conv_b_dw, conv_ln_g, conv_ln_b, conv_w_out, conv_b_out, kv_norm, w_kv, k_norm, w_q, q_norm, w_o, ffn_w_gate, ffn_w_up, ffn_w_down, moe_router, moe_w_gate, moe_w_up, moe_w_down):
    d = x_prompt.shape[-1]
    heads = d // LANES
    assert k_norm.shape[0] == LANES and w_kv.shape[1] == 2 * d
    w = {
        "norm_mix": norm_mix, "norm_ffn": norm_ffn,
        "conv_wa": conv_w_in[:, :, :d].astype(BF16), "conv_wg": conv_w_in[:, :, d:].astype(BF16),
        "conv_ba": conv_b_in[:, None, :d], "conv_bg": conv_b_in[:, None, d:],
        "conv_w_dw": conv_w_dw, "conv_b_dw": conv_b_dw, "conv_ln_g": conv_ln_g, "conv_ln_b": conv_ln_b,
        "conv_w_out": conv_w_out.astype(BF16), "conv_b_out": conv_b_out,
        "kv_norm": kv_norm[None], "wk": w_kv[:, :d].astype(BF16), "wv": w_kv[:, d:].astype(BF16),
        "k_norm": k_norm[None], "w_q": w_q.astype(BF16), "q_norm": q_norm, "w_o": w_o.astype(BF16),
        "ffn_w_gate": ffn_w_gate.astype(BF16), "ffn_w_up": ffn_w_up.astype(BF16),
        "ffn_w_down": ffn_w_down.astype(BF16),
        "moe_router": moe_router, "moe_w_gate": moe_w_gate.astype(BF16),
        "moe_w_up": moe_w_up.astype(BF16), "moe_w_down": moe_w_down.astype(BF16),
    }
    n_conv, width = conv_w_dw.shape[0], conv_w_dw.shape[1]

    bp, sp, _ = x_prompt.shape
    zero_prev = jnp.zeros((n_conv, bp, width - 1, d), x_prompt.dtype)
    y_p, k_p, v_p, conv_p = _trunk(x_prompt.reshape(bp * sp, d), bp, sp, zero_prev, None, w)

    bs, ss, _ = x_sample.shape
    past = cache_k.shape[1]
    cache = (cache_k.reshape(bs, past, d), cache_v.reshape(bs, past, d))
    y_s, k_s, v_s, conv_s = _trunk(x_sample.reshape(bs * ss, d), bs, ss, state_conv, cache, w)

    return (y_p.reshape(bp, sp, d), y_s.reshape(bs, ss, d),
            k_p.reshape(bp, sp, heads, LANES), v_p.reshape(bp, sp, heads, LANES), conv_p,
            k_s.reshape(bs, ss, heads, LANES), v_s.reshape(bs, ss, heads, LANES), conv_s)
```

```python
import functools

import jax
import jax.numpy as jnp
from jax import lax
from jax.experimental import pallas as pl
from jax.experimental.pallas import tpu as pltpu

F32 = jnp.float32
BF16 = jnp.bfloat16

EPS = 1e-6
LANES = 128
HALO = 32
SMEM_CHUNK = 1024
VMEM_LIMIT = 56 * 1024 * 1024
LOG2_E = 1.4426950408889634
LOG_ZERO = -151.0


def _params(*sem):
    return pltpu.CompilerParams(dimension_semantics=sem, vmem_limit_bytes=VMEM_LIMIT)


def _pick(n, pref, mult=8):
    if n <= pref:
        return n
    t = (pref // mult) * mult
    while t > mult and n % t:
        t -= mult
    assert n % t == 0, (n, pref, mult)
    return t


def _rms_bf16(x, g):
    ms = jnp.mean(x * x, axis=-1, keepdims=True)
    return (x * lax.rsqrt(ms + EPS) * g).astype(BF16)


def _dot(a, b):
    return jnp.dot(a, b, preferred_element_type=F32)


def _glu_in_kernel(x_ref, g_ref, wa_ref, wg_ref, ba_ref, bg_ref, u_ref, h_scr):
    @pl.when(pl.program_id(1) == 0)
    def _():
        h_scr[...] = _rms_bf16(x_ref[...], g_ref[...])

    h = h_scr[...]
    a = _dot(h, wa_ref[...]) + ba_ref[...]
    g = _dot(h, wg_ref[...]) + bg_ref[...]
    u_ref[...] = a * jax.nn.sigmoid(g)


def _glu_in(x, gain, wa, wg, ba, bg):
    m, d = x.shape
    n = wa.shape[1]
    tm, tn = _pick(m, 1024), _pick(n, 1024, LANES)
    return pl.pallas_call(
        _glu_in_kernel,
        grid=(m // tm, n // tn),
        in_specs=[
            pl.BlockSpec((tm, d), lambda i, j: (i, 0)),
            pl.BlockSpec((1, d), lambda i, j: (0, 0)),
            pl.BlockSpec((d, tn), lambda i, j: (0, j)),
            pl.BlockSpec((d, tn), lambda i, j: (0, j)),
            pl.BlockSpec((1, tn), lambda i, j: (0, j)),
            pl.BlockSpec((1, tn), lambda i, j: (0, j)),
        ],
        out_specs=pl.BlockSpec((tm, tn), lambda i, j: (i, j)),
        out_shape=jax.ShapeDtypeStruct((m, n), F32),
        scratch_shapes=[pltpu.VMEM((tm, d), BF16)],
        compiler_params=_params("parallel", "arbitrary"),
        name="glu_in",
    )(x, gain, wa, wg, ba, bg)


def _conv_kernel(u_ref, halo_ref, prev_ref, wdw_ref, bdw_ref, lng_ref, lnb_ref, c_ref,
                 ext_scr, sh_scr, acc_scr, *, tc, width, rows_ln):
    d = u_ref.shape[1]
    i = pl.program_id(1)

    @pl.when(i == 0)
    def _():
        ext_scr[0:HALO, :] = prev_ref[0]

    @pl.when(i > 0)
    def _():
        ext_scr[0:HALO, :] = halo_ref[...]

    ext_scr[HALO:HALO + tc, :] = u_ref[...]

    off = HALO - (width - 1)
    n_shift = 8
    rc = min(tc, 64)

    def lane_body(c, carry):
        l0 = pl.multiple_of(c * LANES, LANES)
        for r in range(n_shift):
            taps_r = [w for w in range(width) if w % n_shift == r]
            n_rows = tc + (max(taps_r) - r)
            sh_scr[r, 0:n_rows, :] = ext_scr[off + r:off + r + n_rows, pl.ds(l0, LANES)]
        for r0 in range(0, tc, rc):
            acc = jnp.broadcast_to(bdw_ref[:, pl.ds(l0, LANES)], (rc, LANES))
            for w in range(width):
                r, q = w % n_shift, w // n_shift
                win = sh_scr[r, r0 + 8 * q:r0 + 8 * q + rc, :]
                acc = acc + win * wdw_ref[w:w + 1, pl.ds(l0, LANES)]
            acc_scr[r0:r0 + rc, pl.ds(l0, LANES)] = acc
        return carry

    lax.fori_loop(0, d // LANES, lane_body, 0)

    def ln_body(r, carry):
        r0 = pl.multiple_of(r * rows_ln, rows_ln)
        c = acc_scr[pl.ds(r0, rows_ln), :]
        mu = jnp.mean(c, axis=-1, keepdims=True)
        xc = c - mu
        var = jnp.mean(xc * xc, axis=-1, keepdims=True)
        y = xc * lax.rsqrt(var + EPS) * lng_ref[...] + lnb_ref[...]
        c_ref[pl.ds(r0, rows_ln), :] = (y * jax.nn.sigmoid(y)).astype(c_ref.dtype)
        return carry

    lax.fori_loop(0, tc // rows_ln, ln_body, 0, unroll=min(4, tc // rows_ln))


def _conv_ln_swish(u, prev, wdw, bdw, lng, lnb, batch, seq):
    m, d = u.shape
    width = wdw.shape[0]
    assert width - 1 <= HALO and seq % HALO == 0
    tc = _pick(seq, 256, HALO)
    nt = seq // tc
    hb = tc // HALO
    wdw_p = jnp.pad(wdw, ((0, (-width) % 8), (0, 0)))
    kern = functools.partial(_conv_kernel, tc=tc, width=width, rows_ln=16)
    return pl.pallas_call(
        kern,
        grid=(batch, nt),
        in_specs=[
            pl.BlockSpec((tc, d), lambda b, i: (b * nt + i, 0)),
            pl.BlockSpec((HALO, d), lambda b, i: (jnp.maximum((b * nt + i) * hb - 1, 0), 0)),
            pl.BlockSpec((1, HALO, d), lambda b, i: (b, 0, 0)),
            pl.BlockSpec(wdw_p.shape, lambda b, i: (0, 0)),
            pl.BlockSpec((1, d), lambda b, i: (0, 0)),
            pl.BlockSpec((1, d), lambda b, i: (0, 0)),
            pl.BlockSpec((1, d), lambda b, i: (0, 0)),
        ],
        out_specs=pl.BlockSpec((tc, d), lambda b, i: (b * nt + i, 0)),
        out_shape=jax.ShapeDtypeStruct((m, d), BF16),
        scratch_shapes=[
            pltpu.VMEM((HALO + tc, d), F32),
            pltpu.VMEM((8, tc + 24, LANES), F32),
            pltpu.VMEM((tc, d), F32),
        ],
        compiler_params=_params("parallel", "arbitrary"),
        name="conv_ln_swish",
    )(u, u, prev, wdw_p, bdw, lng, lnb)


def _mm_res_kernel(x_ref, w_ref, b_ref, r_ref, o_ref):
    o_ref[...] = r_ref[...] + b_ref[...] + _dot(x_ref[...], w_ref[...])


def _mm_res(x, w, b, res):
    m, k = x.shape
    n = w.shape[1]
    tm, tn = _pick(m, 1024), _pick(n, 1024, LANES)
    return pl.pallas_call(
        _mm_res_kernel,
        grid=(m // tm, n // tn),
        in_specs=[
            pl.BlockSpec((tm, k), lambda i, j: (i, 0)),
            pl.BlockSpec((k, tn), lambda i, j: (0, j)),
            pl.BlockSpec((1, tn), lambda i, j: (0, j)),
            pl.BlockSpec((tm, tn), lambda i, j: (i, j)),
        ],
        out_specs=pl.BlockSpec((tm, tn), lambda i, j: (i, j)),
        out_shape=jax.ShapeDtypeStruct((m, n), F32),
        compiler_params=_params("parallel", "parallel"),
        name="mm_res",
    )(x, w, b, res)


def _head_rms(y, gain, scale):
    out = []
    for c in range(y.shape[1] // LANES):
        yc = y[:, c * LANES:(c + 1) * LANES]
        ms = jnp.mean(yc * yc, axis=-1, keepdims=True)
        yc = yc * lax.rsqrt(ms + EPS) * gain
        out.append(yc * scale if scale != 1.0 else yc)
    return out


def _q_kernel(x_ref, g_ref, w_ref, qn_ref, q_ref, h_scr, *, scale):
    @pl.when(pl.program_id(1) == 0)
    def _():
        h_scr[...] = _rms_bf16(x_ref[...], g_ref[...])

    y = _dot(h_scr[...], w_ref[...])
    for c, yc in enumerate(_head_rms(y, qn_ref[...], scale)):
        q_ref[:, c * LANES:(c + 1) * LANES] = yc.astype(q_ref.dtype)


def _q_proj(x, gain, w, qn, scale):
    m, d = x.shape
    n = w.shape[1]
    tm, tn = _pick(m, 1024), _pick(n, 1024, LANES)
    return pl.pallas_call(
        functools.partial(_q_kernel, scale=scale),
        grid=(m // tm, n // tn),
        in_specs=[
            pl.BlockSpec((tm, d), lambda i, j: (i, 0)),
            pl.BlockSpec((1, d), lambda i, j: (0, 0)),
            pl.BlockSpec((d, tn), lambda i, j: (0, j)),
            pl.BlockSpec((1, LANES), lambda i, j: (0, 0)),
        ],
        out_specs=pl.BlockSpec((tm, tn), lambda i, j: (i, j)),
        out_shape=jax.ShapeDtypeStruct((m, n), BF16),
        scratch_shapes=[pltpu.VMEM((tm, d), BF16)],
        compiler_params=_params("parallel", "arbitrary"),
        name="q_proj",
    )(x, gain, w, qn)


def _kv_kernel(x_ref, g_ref, wk_ref, wv_ref, kn_ref, k_ref, v_ref, kb_ref, vb_ref, h_scr, *, n_heads):
    j = pl.program_id(1)
    tm = x_ref.shape[0]
    heads_per_step = wk_ref.shape[1] // LANES

    @pl.when(j == 0)
    def _():
        h_scr[...] = _rms_bf16(x_ref[...], g_ref[...])

    h = h_scr[...]
    k_heads = _head_rms(_dot(h, wk_ref[...]), kn_ref[...], 1.0)
    v = _dot(h, wv_ref[...])
    vb_ref[...] = v.astype(BF16)
    for c, kc in enumerate(k_heads):
        kb_ref[:, c * LANES:(c + 1) * LANES] = kc.astype(BF16)
    for step in range(n_heads // heads_per_step):
        @pl.when(j == step)
        def _(step=step):
            for c, kc in enumerate(k_heads):
                head = step * heads_per_step + c
                k_ref[pl.ds(head, tm, stride=n_heads), :] = kc
                v_ref[pl.ds(head, tm, stride=n_heads), :] = v[:, c * LANES:(c + 1) * LANES]


def _kv_proj(x, gain, wk, wv, kn):
    m, d = x.shape
    n = wk.shape[1]
    n_heads = n // LANES
    tm, tn = _pick(m, 512), _pick(n, 512, LANES)
    wide = pl.BlockSpec((tm, tn), lambda i, j: (i, j))
    rows = pl.BlockSpec((tm * n_heads, LANES), lambda i, j: (i, 0))
    return pl.pallas_call(
        functools.partial(_kv_kernel, n_heads=n_heads),
        grid=(m // tm, n // tn),
        in_specs=[
            pl.BlockSpec((tm, d), lambda i, j: (i, 0)),
            pl.BlockSpec((1, d), lambda i, j: (0, 0)),
            pl.BlockSpec((d, tn), lambda i, j: (0, j)),
            pl.BlockSpec((d, tn), lambda i, j: (0, j)),
            pl.BlockSpec((1, LANES), lambda i, j: (0, 0)),
        ],
        out_specs=[rows, rows, wide, wide],
        out_shape=[jax.ShapeDtypeStruct((m * n_heads, LANES), F32), jax.ShapeDtypeStruct((m * n_heads, LANES), F32),
                   jax.ShapeDtypeStruct((m, n), BF16), jax.ShapeDtypeStruct((m, n), BF16)],
        scratch_shapes=[pltpu.VMEM((tm, d), BF16)],
        compiler_params=_params("parallel", "arbitrary"),
        name="kv_proj",
    )(x, gain, wk, wv, kn)


def _strict_lower(n):
    row = lax.broadcasted_iota(jnp.int32, (n, n), 0)
    col = lax.broadcasted_iota(jnp.int32, (n, n), 1)
    return col < row


def _sb_block(q, k, v, tri, later, mask):
    z = lax.dot_general(q, k, (((1,), (1,)), ((), ())), preferred_element_type=F32)
    drop = jnp.maximum(z, 0.0) + LOG2_E * jnp.log(1.0 + jnp.exp2(-jnp.abs(z)))
    if mask is not None:
        drop = jnp.where(mask, drop, 0.0)
    in_blk = _dot(drop.astype(BF16), tri)
    w = jnp.exp2((z - drop) - in_blk + later)
    if mask is not None:
        w = jnp.where(mask, w, 0.0)
    o = _dot(w.astype(BF16), v)
    return o, later - jnp.sum(drop, axis=-1, keepdims=True)


def _sb_sweep(chains, tri_diag, tri, q_ref, acc_scr):
    tq = chains[0][0].stop - chains[0][0].start
    mask = _strict_lower(tq)
    laters = []
    for rows, lanes, diag_kv, _, _ in chains:
        k, v = diag_kv()
        o, later = _sb_block(q_ref[rows, lanes], k, v, tri_diag, jnp.zeros((tq, 1), F32), mask)
        acc_scr[rows, lanes] = o
        laters.append(later)

    def live(j, chain, later):
        n = chain[4]
        return later if isinstance(n, int) else jnp.where(j < n, later, 2 * LOG_ZERO)

    n_max = functools.reduce(jnp.maximum, [c[4] for c in chains])

    def cond(c):
        j = c[0]
        worst = functools.reduce(jnp.maximum, [live(j, ch, lt) for ch, lt in zip(chains, c[1:])])
        return jnp.logical_and(j < n_max, jnp.max(worst) > LOG_ZERO)

    def body(c):
        j = c[0]
        out = [j + 1]
        for chain, later in zip(chains, c[1:]):
            rows, lanes, _, earlier_kv, n = chain
            k, v = earlier_kv(j)
            do, later = _sb_block(q_ref[rows, lanes], k, v, tri, live(j, chain, later), None)
            acc_scr[rows, lanes] += do
            out.append(later)
        return tuple(out)

    lax.while_loop(cond, body, (jnp.int32(0), *laters))


def _attn_self_kernel(q_ref, k_ref, v_ref, o_ref, acc_scr, *, bk, n_sub):
    i = pl.program_id(2)
    tri = jnp.where(_strict_lower(bk), 1.0, 0.0).astype(BF16)

    def load_kv(lanes, kb):
        k0 = pl.multiple_of(kb * bk, bk)
        return k_ref[pl.ds(k0, bk), lanes], v_ref[pl.ds(k0, bk), lanes]

    chains = []
    for s in range(n_sub):
        qb = i * n_sub + s
        for h in range(q_ref.shape[1] // LANES):
            lanes = slice(h * LANES, (h + 1) * LANES)
            chains.append((slice(s * bk, (s + 1) * bk), lanes,
                           functools.partial(load_kv, lanes, qb),
                           lambda j, lanes=lanes, qb=qb: load_kv(lanes, jnp.maximum(qb - 1 - j, 0)),
                           qb))
    _sb_sweep(chains, tri, tri, q_ref, acc_scr)
    o_ref[...] = acc_scr[...].astype(o_ref.dtype)


def _attn_self(q, k, v, batch, seq):
    m, hd = q.shape
    hb = _pick(hd, 4 * LANES, LANES)
    bk = _pick(seq, 256, LANES)
    n_sub = 2 if seq % (2 * bk) == 0 else 1
    bq = n_sub * bk
    nq = seq // bq
    kv_spec = pl.BlockSpec((seq, hb), lambda b, h, i: (b, h), pipeline_mode=pl.Buffered(1))
    return pl.pallas_call(
        functools.partial(_attn_self_kernel, bk=bk, n_sub=n_sub),
        grid=(batch, hd // hb, nq),
        in_specs=[pl.BlockSpec((bq, hb), lambda b, h, i: (b * nq + i, h)), kv_spec, kv_spec],
        out_specs=pl.BlockSpec((bq, hb), lambda b, h, i: (b * nq + i, h)),
        out_shape=jax.ShapeDtypeStruct((m, hd), BF16),
        scratch_shapes=[pltpu.VMEM((bq, hb), F32)],
        compiler_params=_params("parallel", "parallel", "arbitrary"),
        name="attn_self",
    )(q, k, v)


def _attn_cached_kernel(q_ref, kn_ref, vn_ref, ck_ref, cv_ref, o_ref, acc_scr, *, bk):
    tq = q_ref.shape[0]
    past = ck_ref.shape[1]
    tri = jnp.where(_strict_lower(bk), 1.0, 0.0).astype(BF16)

    n_old = past // bk

    def diag_kv(lanes):
        return kn_ref[:, lanes], vn_ref[:, lanes]

    def earlier_kv(lanes, j):
        k0 = pl.multiple_of((n_old - 1 - j) * bk, bk)
        return ck_ref[0, pl.ds(k0, bk), lanes].astype(BF16), cv_ref[0, pl.ds(k0, bk), lanes].astype(BF16)

    chains = []
    for h in range(q_ref.shape[1] // LANES):
        lanes = slice(h * LANES, (h + 1) * LANES)
        chains.append((slice(0, tq), lanes, functools.partial(diag_kv, lanes),
                       functools.partial(earlier_kv, lanes), n_old))
    _sb_sweep(chains, tri[:tq, :tq], tri, q_ref, acc_scr)
    o_ref[...] = acc_scr[...].astype(o_ref.dtype)


def _attn_cached(q, k_new, v_new, cache_k, cache_v, batch, seq):
    m, hd = q.shape
    hb = _pick(hd, 8 * LANES, LANES)
    past = cache_k.shape[1]
    bk = _pick(past, 256, LANES)
    assert seq <= bk and seq % 16 == 0
    new = pl.BlockSpec((seq, hb), lambda b, h: (b, h))
    old = pl.BlockSpec((1, past, hb), lambda b, h: (b, 0, h))
    return pl.pallas_call(
        functools.partial(_attn_cached_kernel, bk=bk),
        grid=(batch, hd // hb),
        in_specs=[new, new, new, old, old],
        out_specs=new,
        out_shape=jax.ShapeDtypeStruct((m, hd), BF16),
        scratch_shapes=[pltpu.VMEM((seq, hb), F32)],
        compiler_params=_params("parallel", "parallel"),
        name="attn_cached",
    )(q, k_new, v_new, cache_k, cache_v)


def _ffn_kernel(x_ref, g_ref, wg_ref, wu_ref, wd_ref, o_ref, h_scr):
    @pl.when(pl.program_id(1) == 0)
    def _():
        x = x_ref[...]
        h_scr[...] = _rms_bf16(x, g_ref[...])
        o_ref[...] = x

    h = h_scr[...]
    act = jax.nn.silu(_dot(h, wg_ref[...])) * _dot(h, wu_ref[...])
    o_ref[...] += _dot(act.astype(BF16), wd_ref[...])


def _ffn(x, gain, wg, wu, wd):
    m, d = x.shape
    f = wg.shape[1]
    tm, tf = _pick(m, 1024), _pick(f, 256, LANES)
    return pl.pallas_call(
        _ffn_kernel,
        grid=(m // tm, f // tf),
        in_specs=[
            pl.BlockSpec((tm, d), lambda i, j: (i, 0)),
            pl.BlockSpec((1, d), lambda i, j: (0, 0)),
            pl.BlockSpec((d, tf), lambda i, j: (0, j)),
            pl.BlockSpec((d, tf), lambda i, j: (0, j)),
            pl.BlockSpec((tf, d), lambda i, j: (j, 0)),
        ],
        out_specs=pl.BlockSpec((tm, d), lambda i, j: (i, 0)),
        out_shape=jax.ShapeDtypeStruct((m, d), F32),
        scratch_shapes=[pltpu.VMEM((tm, d), BF16)],
        compiler_params=_params("parallel", "arbitrary"),
        name="ffn",
    )(x, gain, wg, wu, wd)


def _split_bf16(a):
    hi = a.astype(BF16)
    return hi, (a - hi.astype(F32)).astype(BF16)


def _pack_rows(y, dst_ref):
    tm, d = y.shape
    r = d // (2 * LANES)
    half = d // 2
    for j in range(r):
        hi = pltpu.bitcast(y[:, j * LANES:(j + 1) * LANES].astype(BF16).astype(F32), jnp.uint32)
        lo = pltpu.bitcast(y[:, half + j * LANES:half + (j + 1) * LANES].astype(BF16).astype(F32), jnp.uint32)
        dst_ref[pl.ds(j, tm, stride=r), :] = hi | (lo >> 16)


def _unpack_rows(src_ref, tm, r):
    his, los = [], []
    for j in range(r):
        p = src_ref[pl.ds(j, tm, stride=r), :]
        his.append(pltpu.bitcast(p & jnp.uint32(0xFFFF0000), F32))
        los.append(pltpu.bitcast(p << 16, F32))
    return his, los


def _router_kernel(x_ref, g_ref, wr_ref, hp_ref, route_ref, counts_ref, run_scr, *, n_experts):
    @pl.when(pl.program_id(0) == 0)
    def _():
        run_scr[...] = jnp.zeros_like(run_scr)

    x = x_ref[...]
    tm = x.shape[0]
    ms = jnp.mean(x * x, axis=-1, keepdims=True)
    h = x * lax.rsqrt(ms + EPS) * g_ref[...]
    _pack_rows(h, hp_ref)
    h_hi, h_lo = _split_bf16(h)
    w_hi, w_lo = _split_bf16(wr_ref[...])
    logits = _dot(h_hi, w_hi) + (_dot(h_hi, w_lo) + _dot(h_lo, w_hi))
    lane = lax.broadcasted_iota(jnp.int32, logits.shape, 1).astype(F32)
    neg = -jnp.inf
    l1 = jnp.where(lane < n_experts, logits, neg)
    m1 = jnp.max(l1, axis=-1, keepdims=True)
    i1 = jnp.min(jnp.where(l1 == m1, lane, float(LANES)), axis=-1, keepdims=True)
    l2 = jnp.where(lane == i1, neg, l1)
    m2 = jnp.max(l2, axis=-1, keepdims=True)
    i2 = jnp.min(jnp.where(l2 == m2, lane, float(LANES)), axis=-1, keepdims=True)
    e2 = jnp.exp(m2 - m1)
    den = 1.0 + e2
    hot1 = lane == i1
    hot2 = lane == i2
    both = jnp.where(jnp.logical_or(hot1, hot2), 1.0, 0.0)
    earlier = jnp.where(_strict_lower(tm), 1.0, 0.0).astype(BF16)
    before = run_scr[...] + _dot(earlier, both.astype(BF16))
    rank1 = jnp.sum(jnp.where(hot1, before, 0.0), axis=-1, keepdims=True)
    rank2 = jnp.sum(jnp.where(hot2, before, 0.0), axis=-1, keepdims=True)
    run_scr[...] += jnp.sum(both, axis=0, keepdims=True)
    counts_ref[...] = run_scr[...]
    route = jnp.zeros_like(logits)
    for k, val in enumerate((i1, i2, 1.0 / den, e2 / den, rank1, rank2)):
        route = jnp.where(lane == float(k), val, route)
    route_ref[...] = route


def _router(x, gain, wr):
    m, d = x.shape
    n_experts = wr.shape[1]
    assert n_experts <= LANES and d % (2 * LANES) == 0
    r = d // (2 * LANES)
    wr_p = jnp.pad(wr, ((0, 0), (0, LANES - n_experts)))
    tm = _pick(m, 512)
    return pl.pallas_call(
        functools.partial(_router_kernel, n_experts=n_experts),
        grid=(m // tm,),
        in_specs=[
            pl.BlockSpec((tm, d), lambda i: (i, 0)),
            pl.BlockSpec((1, d), lambda i: (0, 0)),
            pl.BlockSpec((d, LANES), lambda i: (0, 0)),
        ],
        out_specs=[
            pl.BlockSpec((tm * r, LANES), lambda i: (i, 0)),
            pl.BlockSpec((tm, LANES), lambda i: (i, 0)),
            pl.BlockSpec((1, LANES), lambda i: (0, 0)),
        ],
        out_shape=[jax.ShapeDtypeStruct((m * r, LANES), jnp.uint32),
                   jax.ShapeDtypeStruct((m, LANES), F32),
                   jax.ShapeDtypeStruct((1, LANES), F32)],
        scratch_shapes=[pltpu.VMEM((1, LANES), F32)],
        compiler_params=_params("arbitrary"),
        name="router",
    )(x, gain, wr_p)


def _aligned(v, mult):
    return v if isinstance(v, int) else pl.multiple_of(v, mult)


def _moe_ffn_kernel(tile_expert_ref, tile_rows_ref, n_valid_ref, hp_hbm, meta_hbm, wg_ref, wu_ref, wd_ref,
                    y_hbm, xbuf, ybuf, meta_smem, gsem, ssem, msem, *, tm, r, chunk, unroll):
    del tile_expert_ref
    t = pl.program_id(0)
    n_valid = n_valid_ref[0]

    def meta_copy(tile):
        slot = tile % 3
        return pltpu.make_async_copy(meta_hbm.at[pl.ds(_aligned(tile * chunk, chunk), chunk)],
                                     meta_smem.at[pl.ds(_aligned(slot * chunk, chunk), chunk)], msem.at[slot])

    def rows(ref, row, n=1):
        return ref.at[pl.ds(_aligned(row * r, r), n * r), :]

    def gather_copy(tile, row):
        tok = meta_smem[(tile % 3) * chunk + row]
        return pltpu.make_async_copy(rows(hp_hbm, tok), rows(xbuf.at[tile % 2], row), gsem.at[tile % 2])

    def scatter_copy(tile, row):
        dst = meta_smem[(tile % 3) * chunk + tm + row]
        return pltpu.make_async_copy(rows(ybuf.at[tile % 2], row), rows(y_hbm, dst), ssem.at[tile % 2])

    def gather_wait(tile, n):
        pltpu.make_async_copy(rows(hp_hbm, 0, n), rows(xbuf.at[tile % 2], 0, n), gsem.at[tile % 2]).wait()

    def scatter_wait(tile, n):
        pltpu.make_async_copy(rows(ybuf.at[tile % 2], 0, n), rows(y_hbm, 0, n), ssem.at[tile % 2]).wait()

    def for_rows(fn, n=None):
        def body(row, carry):
            fn(row)
            return carry

        if n is None:
            lax.fori_loop(0, tm, body, 0, unroll=unroll)
            return

        def group(g, carry):
            for j in range(unroll):
                fn(g * unroll + j)
            return carry

        lax.fori_loop(0, n // unroll, group, 0)
        lax.fori_loop((n // unroll) * unroll, n, body, 0)

    def wait_rows(wait, n=None):
        def many(g, carry):
            wait(unroll)
            return carry

        def one(g, carry):
            wait(1)
            return carry

        if n is None:
            lax.fori_loop(0, tm // unroll, many, 0)
            return
        lax.fori_loop(0, n // unroll, many, 0)
        lax.fori_loop(0, n % unroll, one, 0)

    def valid_rows(tile):
        return tile_rows_ref[tile]

    @pl.when(t == 0)
    def _():
        meta_copy(0).start()

        @pl.when(n_valid > 1)
        def _():
            meta_copy(1).start()

        meta_copy(0).wait()
        for_rows(lambda row: gather_copy(0, row).start())

    @pl.when(t + 2 < n_valid)
    def _():
        meta_copy(t + 2).start()

    @pl.when(t + 1 < n_valid)
    def _():
        meta_copy(t + 1).wait()
        for_rows(lambda row: gather_copy(t + 1, row).start())

    @pl.when(t < n_valid)
    def _():
        wait_rows(functools.partial(gather_wait, t))

        @pl.when(t >= 2)
        def _():
            wait_rows(functools.partial(scatter_wait, t - 2), valid_rows(t - 2))

        his, los = _unpack_rows(xbuf.at[t % 2], tm, r)
        x = jnp.concatenate([p.astype(BF16) for p in his + los], axis=1)
        act = jax.nn.silu(_dot(x, wg_ref[0])) * _dot(x, wu_ref[0])
        _pack_rows(_dot(act.astype(BF16), wd_ref[0]), ybuf.at[t % 2])
        for_rows(lambda row: scatter_copy(t, row).start(), valid_rows(t))

    @pl.when(t == n_valid - 1)
    def _():
        @pl.when(t >= 1)
        def _():
            wait_rows(functools.partial(scatter_wait, t - 1), valid_rows(t - 1))

        wait_rows(functools.partial(scatter_wait, t), valid_rows(t))


def _combine_kernel(x_ref, route_ref, y0_ref, y1_ref, o_ref, *, r):
    tm, d = x_ref.shape
    half = d // 2
    g0 = route_ref[:, 2:3]
    g1 = route_ref[:, 3:4]
    h0, l0 = _unpack_rows(y0_ref, tm, r)
    h1, l1 = _unpack_rows(y1_ref, tm, r)
    for j in range(r):
        a = slice(j * LANES, (j + 1) * LANES)
        b = slice(half + j * LANES, half + (j + 1) * LANES)
        o_ref[:, a] = x_ref[:, a] + (g0 * h0[j] + g1 * h1[j])
        o_ref[:, b] = x_ref[:, b] + (g0 * l0[j] + g1 * l1[j])


def _moe(x, gain, wr, wg, wu, wd):
    m, d = x.shape
    n_experts, _, f = wg.shape
    r = d // (2 * LANES)
    hp, route, counts = _router(x, gain, wr)

    tm = _pick(m, min(512, max(256, m // (2 * n_experts))))
    n_tiles = (2 * m) // tm + n_experts
    n_rows = n_tiles * tm
    expert = route[:, 0:2].astype(jnp.int32)
    rank = route[:, 4:6].astype(jnp.int32)
    cnt = counts[0, :n_experts].astype(jnp.int32)
    tiles_e = (cnt + tm - 1) // tm
    tile_end = jnp.cumsum(tiles_e)
    n_valid = tile_end[-1]
    tile_start = tile_end - tiles_e
    hot = expert[..., None] == jnp.arange(n_experts, dtype=jnp.int32)
    pos = (jnp.sum(jnp.where(hot, tile_start, 0), axis=-1) * tm + rank).reshape(-1)
    row_assign = jnp.zeros((n_rows,), jnp.int32).at[pos].set(jnp.arange(2 * m, dtype=jnp.int32))
    row_token = (row_assign >> 1).reshape(n_tiles, tm)
    row_dest = ((row_assign & 1) * m + (row_assign >> 1)).reshape(n_tiles, tm)
    chunk = -(-2 * tm // SMEM_CHUNK) * SMEM_CHUNK
    meta = jnp.pad(jnp.concatenate([row_token, row_dest], axis=1), ((0, 0), (0, chunk - 2 * tm))).reshape(-1)
    tile_ids = jnp.arange(n_tiles, dtype=jnp.int32)
    tile_expert = jnp.searchsorted(tile_end, jnp.minimum(tile_ids, n_valid - 1), side="right").astype(jnp.int32)
    tile_rows = jnp.clip(cnt[tile_expert] - (tile_ids - tile_start[tile_expert]) * tm, 0, tm)
    tile_rows = jnp.where(tile_ids < n_valid, tile_rows, 0).astype(jnp.int32)

    y = pl.pallas_call(
        functools.partial(_moe_ffn_kernel, tm=tm, r=r, chunk=chunk, unroll=8),
        grid_spec=pltpu.PrefetchScalarGridSpec(
            num_scalar_prefetch=3,
            grid=(n_tiles,),
            in_specs=[
                pl.BlockSpec(memory_space=pl.ANY),
                pl.BlockSpec(memory_space=pl.ANY),
                pl.BlockSpec((1, d, f), lambda t, te, tr, nv: (te[t], 0, 0)),
                pl.BlockSpec((1, d, f), lambda t, te, tr, nv: (te[t], 0, 0)),
                pl.BlockSpec((1, f, d), lambda t, te, tr, nv: (te[t], 0, 0)),
            ],
            out_specs=pl.BlockSpec(memory_space=pl.ANY),
            scratch_shapes=[
                pltpu.VMEM((2, tm * r, LANES), jnp.uint32),
                pltpu.VMEM((2, tm * r, LANES), jnp.uint32),
                pltpu.SMEM((3 * chunk,), jnp.int32),
                pltpu.SemaphoreType.DMA((2,)),
                pltpu.SemaphoreType.DMA((2,)),
                pltpu.SemaphoreType.DMA((3,)),
            ],
        ),
        out_shape=jax.ShapeDtypeStruct((2 * m * r, LANES), jnp.uint32),
        compiler_params=_params("arbitrary"),
        name="moe_ffn",
    )(tile_expert, tile_rows, n_valid.reshape(1), hp, meta, wg, wu, wd)

    tc = _pick(m, 512)
    nb = m // tc
    return pl.pallas_call(
        functools.partial(_combine_kernel, r=r),
        grid=(nb,),
        in_specs=[
            pl.BlockSpec((tc, d), lambda i: (i, 0)),
            pl.BlockSpec((tc, LANES), lambda i: (i, 0)),
            pl.BlockSpec((tc * r, LANES), lambda i: (i, 0)),
            pl.BlockSpec((tc * r, LANES), lambda i: (nb + i, 0)),
        ],
        out_specs=pl.BlockSpec((tc, d), lambda i: (i, 0)),
        out_shape=jax.ShapeDtypeStruct((m, d), F32),
        compiler_params=_params("parallel"),
        name="moe_combine",
    )(x, route, y, y)


def _trunk(x, batch, seq, conv_prev, cache, w):
    d = x.shape[1]
    depth = w["norm_mix"].shape[0]
    n_conv = w["conv_w_dw"].shape[0]
    width = w["conv_w_dw"].shape[1]
    scale = LOG2_E * float(LANES) ** -0.5
    states = []
    k_f32 = v_f32 = k_b = v_b = None
    for layer in range(depth):
        if layer == n_conv:
            k_f32, v_f32, k_b, v_b = _kv_proj(x, w["kv_norm"], w["wk"], w["wv"], w["k_norm"])
        gain = w["norm_mix"][layer:layer + 1]
        if layer < n_conv:
            u = _glu_in(x, gain, w["conv_wa"][layer], w["conv_wg"][layer],
                        w["conv_ba"][layer], w["conv_bg"][layer])
            prev = conv_prev[layer]
            u3 = u.reshape(batch, seq, d)
            keep = width - 1
            if seq >= keep:
                states.append(u3[:, seq - keep:])
            else:
                states.append(jnp.concatenate([prev[:, seq:], u3], axis=1))
            prev_p = jnp.pad(prev, ((0, 0), (HALO - keep, 0), (0, 0)))
            c = _conv_ln_swish(u, prev_p, w["conv_w_dw"][layer], w["conv_b_dw"][layer:layer + 1],
                               w["conv_ln_g"][layer:layer + 1], w["conv_ln_b"][layer:layer + 1], batch, seq)
            x = _mm_res(c, w["conv_w_out"][layer], w["conv_b_out"][layer:layer + 1], x)
        else:
            j = layer - n_conv
            q = _q_proj(x, gain, w["w_q"][j], w["q_norm"][j:j + 1], scale)
            if cache is None:
                o = _attn_self(q, k_b, v_b, batch, seq)
            else:
                o = _attn_cached(q, k_b, v_b, cache[0], cache[1], batch, seq)
            x = _mm_res(o, w["w_o"][j], jnp.zeros((1, d), F32), x)
        gain = w["norm_ffn"][layer:layer + 1]
        i = layer // 2
        if layer % 2 == 0:
            x = _ffn(x, gain, w["ffn_w_gate"][i], w["ffn_w_up"][i], w["ffn_w_down"][i])
        else:
            x = _moe(x, gain, w["moe_router"][i], w["moe_w_gate"][i], w["moe_w_up"][i], w["moe_w_down"][i])
    return x, k_f32, v_f32, jnp.stack(states)


def kernel(x_prompt, x_sample, cache_k, cache_v, state_conv, norm_mix, norm_ffn, conv_w_in, conv_b_in, conv_w_dw, conv_b_dw, conv_ln_g, conv_ln_b, conv_w_out, conv_b_out, kv_norm, w_kv, k_norm, w_q, q_norm, w_o, ffn_w_gate, ffn_w_up, ffn_w_down, moe_router, moe_w_gate, moe_w_up, moe_w_down):
    d = x_prompt.shape[-1]
    heads = d // LANES
    assert k_norm.shape[0] == LANES and w_kv.shape[1] == 2 * d
    w = {
        "norm_mix": norm_mix, "norm_ffn": norm_ffn,
        "conv_wa": conv_w_in[:, :, :d].astype(BF16), "conv_wg": conv_w_in[:, :, d:].astype(BF16),
        "conv_ba": conv_b_in[:, None, :d], "conv_bg": conv_b_in[:, None, d:],
        "conv_w_dw": conv_w_dw, "conv_b_dw": conv_b_dw, "conv_ln_g": conv_ln_g, "conv_ln_b": conv_ln_b,
        "conv_w_out": conv_w_out.astype(BF16), "conv_b_out": conv_b_out,
        "kv_norm": kv_norm[None], "wk": w_kv[:, :d].astype(BF16), "wv": w_kv[:, d:].astype(BF16),
        "k_norm": k_norm[None], "w_q": w_q.astype(BF16), "q_norm": q_norm, "w_o": w_o.astype(BF16),
        "ffn_w_gate": ffn_w_gate.astype(BF16), "ffn_w_up": ffn_w_up.astype(BF16),
        "ffn_w_down": ffn_w_down.astype(BF16),
        "moe_router": moe_router, "moe_w_gate": moe_w_gate.astype(BF16),
        "moe_w_up": moe_w_up.astype(BF16), "moe_w_down": moe_w_down.astype(BF16),
    }
    n_conv, width = conv_w_dw.shape[0], conv_w_dw.shape[1]

    bp, sp, _ = x_prompt.shape
    zero_prev = jnp.zeros((n_conv, bp, width - 1, d), x_prompt.dtype)
    y_p, k_p, v_p, conv_p = _trunk(x_prompt.reshape(bp * sp, d), bp, sp, zero_prev, None, w)

    bs, ss, _ = x_sample.shape
    past = cache_k.shape[1]
    cache = (cache_k.reshape(bs, past, d), cache_v.reshape(bs, past, d))
    y_s, k_s, v_s, conv_s = _trunk(x_sample.reshape(bs * ss, d), bs, ss, state_conv, cache, w)

    return (y_p.reshape(bp, sp, d), y_s.reshape(bs, ss, d),
            k_p.reshape(bp, sp, heads, LANES), v_p.reshape(bp, sp, heads, LANES), conv_p,
            k_s.reshape(bs, ss, heads, LANES), v_s.reshape(bs, ss, heads, LANES), conv_s)
```

```python
import functools

import jax
import jax.numpy as jnp
from jax import lax
from jax.experimental import pallas as pl
from jax.experimental.pallas import tpu as pltpu

F32 = jnp.float32
BF16 = jnp.bfloat16

EPS = 1e-6
LANES = 128
HALO = 32
SMEM_CHUNK = 1024
VMEM_LIMIT = 56 * 1024 * 1024
LOG2_E = 1.4426950408889634
LOG_ZERO = -151.0


def _params(*sem):
    return pltpu.CompilerParams(dimension_semantics=sem, vmem_limit_bytes=VMEM_LIMIT)


def _pick(n, pref, mult=8):
    if n <= pref:
        return n
    t = (pref // mult) * mult
    while t > mult and n % t:
        t -= mult
    assert n % t == 0, (n, pref, mult)
    return t


def _rms_bf16(x, g):
    ms = jnp.mean(x * x, axis=-1, keepdims=True)
    return (x * lax.rsqrt(ms + EPS) * g).astype(BF16)


def _dot(a, b):
    return jnp.dot(a, b, preferred_element_type=F32)


def _glu_in_kernel(x_ref, g_ref, wa_ref, wg_ref, ba_ref, bg_ref, u_ref, h_scr):
    @pl.when(pl.program_id(1) == 0)
    def _():
        h_scr[...] = _rms_bf16(x_ref[...], g_ref[...])

    h = h_scr[...]
    a = _dot(h, wa_ref[...]) + ba_ref[...]
    g = _dot(h, wg_ref[...]) + bg_ref[...]
    u_ref[...] = a * jax.nn.sigmoid(g)


def _glu_in(x, gain, wa, wg, ba, bg):
    m, d = x.shape
    n = wa.shape[1]
    tm, tn = _pick(m, 1024), _pick(n, 1024, LANES)
    return pl.pallas_call(
        _glu_in_kernel,
        grid=(m // tm, n // tn),
        in_specs=[
            pl.BlockSpec((tm, d), lambda i, j: (i, 0)),
            pl.BlockSpec((1, d), lambda i, j: (0, 0)),
            pl.BlockSpec((d, tn), lambda i, j: (0, j)),
            pl.BlockSpec((d, tn), lambda i, j: (0, j)),
            pl.BlockSpec((1, tn), lambda i, j: (0, j)),
            pl.BlockSpec((1, tn), lambda i, j: (0, j)),
        ],
        out_specs=pl.BlockSpec((tm, tn), lambda i, j: (i, j)),
        out_shape=jax.ShapeDtypeStruct((m, n), F32),
        scratch_shapes=[pltpu.VMEM((tm, d), BF16)],
        compiler_params=_params("parallel", "arbitrary"),
        name="glu_in",
    )(x, gain, wa, wg, ba, bg)


def _conv_kernel(u_ref, halo_ref, prev_ref, wdw_ref, bdw_ref, lng_ref, lnb_ref, c_ref,
                 ext_scr, sh_scr, acc_scr, *, tc, width, rows_ln):
    d = u_ref.shape[1]
    i = pl.program_id(1)

    @pl.when(i == 0)
    def _():
        ext_scr[0:HALO, :] = prev_ref[0]

    @pl.when(i > 0)
    def _():
        ext_scr[0:HALO, :] = halo_ref[...]

    ext_scr[HALO:HALO + tc, :] = u_ref[...]

    off = HALO - (width - 1)
    n_shift = 8
    rc = min(tc, 64)

    def lane_body(c, carry):
        l0 = pl.multiple_of(c * LANES, LANES)
        for r in range(n_shift):
            taps_r = [w for w in range(width) if w % n_shift == r]
            n_rows = tc + (max(taps_r) - r)
            sh_scr[r, 0:n_rows, :] = ext_scr[off + r:off + r + n_rows, pl.ds(l0, LANES)]
        for r0 in range(0, tc, rc):
            acc = jnp.broadcast_to(bdw_ref[:, pl.ds(l0, LANES)], (rc, LANES))
            for w in range(width):
                r, q = w % n_shift, w // n_shift
                win = sh_scr[r, r0 + 8 * q:r0 + 8 * q + rc, :]
                acc = acc + win * wdw_ref[w:w + 1, pl.ds(l0, LANES)]
            acc_scr[r0:r0 + rc, pl.ds(l0, LANES)] = acc
        return carry

    lax.fori_loop(0, d // LANES, lane_body, 0)

    def ln_body(r, carry):
        r0 = pl.multiple_of(r * rows_ln, rows_ln)
        c = acc_scr[pl.ds(r0, rows_ln), :]
        mu = jnp.mean(c, axis=-1, keepdims=True)
        xc = c - mu
        var = jnp.mean(xc * xc, axis=-1, keepdims=True)
        y = xc * lax.rsqrt(var + EPS) * lng_ref[...] + lnb_ref[...]
        c_ref[pl.ds(r0, rows_ln), :] = (y * jax.nn.sigmoid(y)).astype(c_ref.dtype)
        return carry

    lax.fori_loop(0, tc // rows_ln, ln_body, 0, unroll=min(4, tc // rows_ln))


def _conv_ln_swish(u, prev, wdw, bdw, lng, lnb, batch, seq):
    m, d = u.shape
    width = wdw.shape[0]
    assert width - 1 <= HALO and seq % HALO == 0
    tc = _pick(seq, 256, HALO)
    nt = seq // tc
    hb = tc // HALO
    wdw_p = jnp.pad(wdw, ((0, (-width) % 8), (0, 0)))
    kern = functools.partial(_conv_kernel, tc=tc, width=width, rows_ln=16)
    return pl.pallas_call(
        kern,
        grid=(batch, nt),
        in_specs=[
            pl.BlockSpec((tc, d), lambda b, i: (b * nt + i, 0)),
            pl.BlockSpec((HALO, d), lambda b, i: (jnp.maximum((b * nt + i) * hb - 1, 0), 0)),
            pl.BlockSpec((1, HALO, d), lambda b, i: (b, 0, 0)),
            pl.BlockSpec(wdw_p.shape, lambda b, i: (0, 0)),
            pl.BlockSpec((1, d), lambda b, i: (0, 0)),
            pl.BlockSpec((1, d), lambda b, i: (0, 0)),
            pl.BlockSpec((1, d), lambda b, i: (0, 0)),
        ],
        out_specs=pl.BlockSpec((tc, d), lambda b, i: (b * nt + i, 0)),
        out_shape=jax.ShapeDtypeStruct((m, d), BF16),
        scratch_shapes=[
            pltpu.VMEM((HALO + tc, d), F32),
            pltpu.VMEM((8, tc + 24, LANES), F32),
            pltpu.VMEM((tc, d), F32),
        ],
        compiler_params=_params("parallel", "arbitrary"),
        name="conv_ln_swish",
    )(u, u, prev, wdw_p, bdw, lng, lnb)


def _mm_res_kernel(x_ref, w_ref, b_ref, r_ref, o_ref):
    o_ref[...] = r_ref[...] + b_ref[...] + _dot(x_ref[...], w_ref[...])


def _mm_res(x, w, b, res):
    m, k = x.shape
    n = w.shape[1]
    tm, tn = _pick(m, 1024), _pick(n, 1024, LANES)
    return pl.pallas_call(
        _mm_res_kernel,
        grid=(m // tm, n // tn),
        in_specs=[
            pl.BlockSpec((tm, k), lambda i, j: (i, 0)),
            pl.BlockSpec((k, tn), lambda i, j: (0, j)),
            pl.BlockSpec((1, tn), lambda i, j: (0, j)),
            pl.BlockSpec((tm, tn), lambda i, j: (i, j)),
        ],
        out_specs=pl.BlockSpec((tm, tn), lambda i, j: (i, j)),
        out_shape=jax.ShapeDtypeStruct((m, n), F32),
        compiler_params=_params("parallel", "parallel"),
        name="mm_res",
    )(x, w, b, res)


def _head_rms(y, gain, scale):
    out = []
    for c in range(y.shape[1] // LANES):
        yc = y[:, c * LANES:(c + 1) * LANES]
        ms = jnp.mean(yc * yc, axis=-1, keepdims=True)
        yc = yc * lax.rsqrt(ms + EPS) * gain
        out.append(yc * scale if scale != 1.0 else yc)
    return out


def _q_kernel(x_ref, g_ref, w_ref, qn_ref, q_ref, h_scr, *, scale):
    @pl.when(pl.program_id(1) == 0)
    def _():
        h_scr[...] = _rms_bf16(x_ref[...], g_ref[...])

    y = _dot(h_scr[...], w_ref[...])
    for c, yc in enumerate(_head_rms(y, qn_ref[...], scale)):
        q_ref[:, c * LANES:(c + 1) * LANES] = yc.astype(q_ref.dtype)


def _q_proj(x, gain, w, qn, scale):
    m, d = x.shape
    n = w.shape[1]
    tm, tn = _pick(m, 1024), _pick(n, 1024, LANES)
    return pl.pallas_call(
        functools.partial(_q_kernel, scale=scale),
        grid=(m // tm, n // tn),
        in_specs=[
            pl.BlockSpec((tm, d), lambda i, j: (i, 0)),
            pl.BlockSpec((1, d), lambda i, j: (0, 0)),
            pl.BlockSpec((d, tn), lambda i, j: (0, j)),
            pl.BlockSpec((1, LANES), lambda i, j: (0, 0)),
        ],
        out_specs=pl.BlockSpec((tm, tn), lambda i, j: (i, j)),
        out_shape=jax.ShapeDtypeStruct((m, n), BF16),
        scratch_shapes=[pltpu.VMEM((tm, d), BF16)],
        compiler_params=_params("parallel", "arbitrary"),
        name="q_proj",
    )(x, gain, w, qn)


def _kv_kernel(x_ref, g_ref, wk_ref, wv_ref, kn_ref, k_ref, v_ref, kb_ref, vb_ref, h_scr):
    @pl.when(pl.program_id(1) == 0)
    def _():
        h_scr[...] = _rms_bf16(x_ref[...], g_ref[...])

    h = h_scr[...]
    k = _dot(h, wk_ref[...])
    for c, kc in enumerate(_head_rms(k, kn_ref[...], 1.0)):
        k_ref[:, c * LANES:(c + 1) * LANES] = kc
        kb_ref[:, c * LANES:(c + 1) * LANES] = kc.astype(BF16)
    v = _dot(h, wv_ref[...])
    v_ref[...] = v
    vb_ref[...] = v.astype(BF16)


def _kv_proj(x, gain, wk, wv, kn):
    m, d = x.shape
    n = wk.shape[1]
    tm, tn = _pick(m, 1024), _pick(n, 512, LANES)
    blk = pl.BlockSpec((tm, tn), lambda i, j: (i, j))
    return pl.pallas_call(
        _kv_kernel,
        grid=(m // tm, n // tn),
        in_specs=[
            pl.BlockSpec((tm, d), lambda i, j: (i, 0)),
            pl.BlockSpec((1, d), lambda i, j: (0, 0)),
            pl.BlockSpec((d, tn), lambda i, j: (0, j)),
            pl.BlockSpec((d, tn), lambda i, j: (0, j)),
            pl.BlockSpec((1, LANES), lambda i, j: (0, 0)),
        ],
        out_specs=[blk, blk, blk, blk],
        out_shape=[jax.ShapeDtypeStruct((m, n), F32), jax.ShapeDtypeStruct((m, n), F32),
                   jax.ShapeDtypeStruct((m, n), BF16), jax.ShapeDtypeStruct((m, n), BF16)],
        scratch_shapes=[pltpu.VMEM((tm, d), BF16)],
        compiler_params=_params("parallel", "arbitrary"),
        name="kv_proj",
    )(x, gain, wk, wv, kn)


def _strict_lower(n):
    row = lax.broadcasted_iota(jnp.int32, (n, n), 0)
    col = lax.broadcasted_iota(jnp.int32, (n, n), 1)
    return col < row


def _sb_block(q, k, v, tri, later, mask):
    z = lax.dot_general(q, k, (((1,), (1,)), ((), ())), preferred_element_type=F32)
    drop = jnp.maximum(z, 0.0) + LOG2_E * jnp.log(1.0 + jnp.exp2(-jnp.abs(z)))
    if mask is not None:
        drop = jnp.where(mask, drop, 0.0)
    in_blk = _dot(drop.astype(BF16), tri)
    w = jnp.exp2((z - drop) - in_blk + later)
    if mask is not None:
        w = jnp.where(mask, w, 0.0)
    o = _dot(w.astype(BF16), v)
    return o, later - jnp.sum(drop, axis=-1, keepdims=True)


def _sb_sweep(chains, tri_diag, tri, q_ref, acc_scr):
    tq = chains[0][0].stop - chains[0][0].start
    mask = _strict_lower(tq)
    laters = []
    for rows, lanes, diag_kv, _, _ in chains:
        k, v = diag_kv()
        o, later = _sb_block(q_ref[rows, lanes], k, v, tri_diag, jnp.zeros((tq, 1), F32), mask)
        acc_scr[rows, lanes] = o
        laters.append(later)

    def live(j, chain, later):
        n = chain[4]
        return later if isinstance(n, int) else jnp.where(j < n, later, 2 * LOG_ZERO)

    n_max = functools.reduce(jnp.maximum, [c[4] for c in chains])

    def cond(c):
        j = c[0]
        worst = functools.reduce(jnp.maximum, [live(j, ch, lt) for ch, lt in zip(chains, c[1:])])
        return jnp.logical_and(j < n_max, jnp.max(worst) > LOG_ZERO)

    def body(c):
        j = c[0]
        out = [j + 1]
        for chain, later in zip(chains, c[1:]):
            rows, lanes, _, earlier_kv, n = chain
            k, v = earlier_kv(j)
            do, later = _sb_block(q_ref[rows, lanes], k, v, tri, live(j, chain, later), None)
            acc_scr[rows, lanes] += do
            out.append(later)
        return tuple(out)

    lax.while_loop(cond, body, (jnp.int32(0), *laters))


def _attn_self_kernel(q_ref, k_ref, v_ref, o_ref, acc_scr, *, bk, n_sub):
    i = pl.program_id(2)
    tri = jnp.where(_strict_lower(bk), 1.0, 0.0).astype(BF16)

    def load_kv(lanes, kb):
        k0 = pl.multiple_of(kb * bk, bk)
        return k_ref[pl.ds(k0, bk), lanes], v_ref[pl.ds(k0, bk), lanes]

    chains = []
    for s in range(n_sub):
        qb = i * n_sub + s
        for h in range(q_ref.shape[1] // LANES):
            lanes = slice(h * LANES, (h + 1) * LANES)
            chains.append((slice(s * bk, (s + 1) * bk), lanes,
                           functools.partial(load_kv, lanes, qb),
                           lambda j, lanes=lanes, qb=qb: load_kv(lanes, jnp.maximum(qb - 1 - j, 0)),
                           qb))
    _sb_sweep(chains, tri, tri, q_ref, acc_scr)
    o_ref[...] = acc_scr[...].astype(o_ref.dtype)


def _attn_self(q, k, v, batch, seq):
    m, hd = q.shape
    hb = _pick(hd, 4 * LANES, LANES)
    bk = _pick(seq, 256, LANES)
    n_sub = 2 if seq % (2 * bk) == 0 else 1
    bq = n_sub * bk
    nq = seq // bq
    kv_spec = pl.BlockSpec((seq, hb), lambda b, h, i: (b, h), pipeline_mode=pl.Buffered(1))
    return pl.pallas_call(
        functools.partial(_attn_self_kernel, bk=bk, n_sub=n_sub),
        grid=(batch, hd // hb, nq),
        in_specs=[pl.BlockSpec((bq, hb), lambda b, h, i: (b * nq + i, h)), kv_spec, kv_spec],
        out_specs=pl.BlockSpec((bq, hb), lambda b, h, i: (b * nq + i, h)),
        out_shape=jax.ShapeDtypeStruct((m, hd), BF16),
        scratch_shapes=[pltpu.VMEM((bq, hb), F32)],
        compiler_params=_params("parallel", "parallel", "arbitrary"),
        name="attn_self",
    )(q, k, v)


def _attn_cached_kernel(q_ref, kn_ref, vn_ref, ck_ref, cv_ref, o_ref, acc_scr, *, bk):
    tq = q_ref.shape[0]
    n_heads = q_ref.shape[1] // LANES
    n_old = ck_ref.shape[1]
    tri = jnp.where(_strict_lower(bk), 1.0, 0.0).astype(BF16)

    def diag_kv(lanes):
        return kn_ref[:, lanes], vn_ref[:, lanes]

    def earlier_kv(h, j):
        rows = pl.ds(h, bk, stride=n_heads)
        return (ck_ref[0, n_old - 1 - j, rows, :].astype(BF16), cv_ref[0, n_old - 1 - j, rows, :].astype(BF16))

    chains = []
    for h in range(n_heads):
        lanes = slice(h * LANES, (h + 1) * LANES)
        chains.append((slice(0, tq), lanes, functools.partial(diag_kv, lanes),
                       functools.partial(earlier_kv, h), n_old))
    _sb_sweep(chains, tri[:tq, :tq], tri, q_ref, acc_scr)
    o_ref[...] = acc_scr[...].astype(o_ref.dtype)


def _attn_cached(q, k_new, v_new, cache_k, cache_v, batch, seq):
    m, hd = q.shape
    _, past, n_heads, _ = cache_k.shape
    assert n_heads * LANES == hd
    bk = _pick(past, 256, LANES)
    assert seq <= bk and seq % 16 == 0
    blocked = (batch, past // bk, bk * n_heads, LANES)
    new = pl.BlockSpec((seq, hd), lambda b: (b, 0))
    old = pl.BlockSpec((1,) + blocked[1:], lambda b: (b, 0, 0, 0))
    return pl.pallas_call(
        functools.partial(_attn_cached_kernel, bk=bk),
        grid=(batch,),
        in_specs=[new, new, new, old, old],
        out_specs=new,
        out_shape=jax.ShapeDtypeStruct((m, hd), BF16),
        scratch_shapes=[pltpu.VMEM((seq, hd), F32)],
        compiler_params=_params("parallel"),
        name="attn_cached",
    )(q, k_new, v_new, cache_k.reshape(blocked), cache_v.reshape(blocked))


def _ffn_kernel(x_ref, g_ref, wg_ref, wu_ref, wd_ref, o_ref, h_scr):
    @pl.when(pl.program_id(1) == 0)
    def _():
        x = x_ref[...]
        h_scr[...] = _rms_bf16(x, g_ref[...])
        o_ref[...] = x

    h = h_scr[...]
    act = jax.nn.silu(_dot(h, wg_ref[...])) * _dot(h, wu_ref[...])
    o_ref[...] += _dot(act.astype(BF16), wd_ref[...])


def _ffn(x, gain, wg, wu, wd):
    m, d = x.shape
    f = wg.shape[1]
    tm, tf = _pick(m, 1024), _pick(f, 256, LANES)
    return pl.pallas_call(
        _ffn_kernel,
        grid=(m // tm, f // tf),
        in_specs=[
            pl.BlockSpec((tm, d), lambda i, j: (i, 0)),
            pl.BlockSpec((1, d), lambda i, j: (0, 0)),
            pl.BlockSpec((d, tf), lambda i, j: (0, j)),
            pl.BlockSpec((d, tf), lambda i, j: (0, j)),
            pl.BlockSpec((tf, d), lambda i, j: (j, 0)),
        ],
        out_specs=pl.BlockSpec((tm, d), lambda i, j: (i, 0)),
        out_shape=jax.ShapeDtypeStruct((m, d), F32),
        scratch_shapes=[pltpu.VMEM((tm, d), BF16)],
        compiler_params=_params("parallel", "arbitrary"),
        name="ffn",
    )(x, gain, wg, wu, wd)


def _split_bf16(a):
    hi = a.astype(BF16)
    return hi, (a - hi.astype(F32)).astype(BF16)


def _pack_rows(y, dst_ref):
    tm, d = y.shape
    r = d // (2 * LANES)
    half = d // 2
    for j in range(r):
        hi = pltpu.bitcast(y[:, j * LANES:(j + 1) * LANES].astype(BF16).astype(F32), jnp.uint32)
        lo = pltpu.bitcast(y[:, half + j * LANES:half + (j + 1) * LANES].astype(BF16).astype(F32), jnp.uint32)
        dst_ref[pl.ds(j, tm, stride=r), :] = hi | (lo >> 16)


def _unpack_rows(src_ref, tm, r):
    his, los = [], []
    for j in range(r):
        p = src_ref[pl.ds(j, tm, stride=r), :]
        his.append(pltpu.bitcast(p & jnp.uint32(0xFFFF0000), F32))
        los.append(pltpu.bitcast(p << 16, F32))
    return his, los


def _router_kernel(x_ref, g_ref, wr_ref, hp_ref, route_ref, counts_ref, run_scr, *, n_experts):
    @pl.when(pl.program_id(0) == 0)
    def _():
        run_scr[...] = jnp.zeros_like(run_scr)

    x = x_ref[...]
    tm = x.shape[0]
    ms = jnp.mean(x * x, axis=-1, keepdims=True)
    h = x * lax.rsqrt(ms + EPS) * g_ref[...]
    _pack_rows(h, hp_ref)
    h_hi, h_lo = _split_bf16(h)
    w_hi, w_lo = _split_bf16(wr_ref[...])
    logits = _dot(h_hi, w_hi) + (_dot(h_hi, w_lo) + _dot(h_lo, w_hi))
    lane = lax.broadcasted_iota(jnp.int32, logits.shape, 1).astype(F32)
    neg = -jnp.inf
    l1 = jnp.where(lane < n_experts, logits, neg)
    m1 = jnp.max(l1, axis=-1, keepdims=True)
    i1 = jnp.min(jnp.where(l1 == m1, lane, float(LANES)), axis=-1, keepdims=True)
    l2 = jnp.where(lane == i1, neg, l1)
    m2 = jnp.max(l2, axis=-1, keepdims=True)
    i2 = jnp.min(jnp.where(l2 == m2, lane, float(LANES)), axis=-1, keepdims=True)
    e2 = jnp.exp(m2 - m1)
    den = 1.0 + e2
    hot1 = lane == i1
    hot2 = lane == i2
    both = jnp.where(jnp.logical_or(hot1, hot2), 1.0, 0.0)
    earlier = jnp.where(_strict_lower(tm), 1.0, 0.0).astype(BF16)
    before = run_scr[...] + _dot(earlier, both.astype(BF16))
    rank1 = jnp.sum(jnp.where(hot1, before, 0.0), axis=-1, keepdims=True)
    rank2 = jnp.sum(jnp.where(hot2, before, 0.0), axis=-1, keepdims=True)
    run_scr[...] += jnp.sum(both, axis=0, keepdims=True)
    counts_ref[...] = run_scr[...]
    route = jnp.zeros_like(logits)
    for k, val in enumerate((i1, i2, 1.0 / den, e2 / den, rank1, rank2)):
        route = jnp.where(lane == float(k), val, route)
    route_ref[...] = route


def _router(x, gain, wr):
    m, d = x.shape
    n_experts = wr.shape[1]
    assert n_experts <= LANES and d % (2 * LANES) == 0
    r = d // (2 * LANES)
    wr_p = jnp.pad(wr, ((0, 0), (0, LANES - n_experts)))
    tm = _pick(m, 512)
    return pl.pallas_call(
        functools.partial(_router_kernel, n_experts=n_experts),
        grid=(m // tm,),
        in_specs=[
            pl.BlockSpec((tm, d), lambda i: (i, 0)),
            pl.BlockSpec((1, d), lambda i: (0, 0)),
            pl.BlockSpec((d, LANES), lambda i: (0, 0)),
        ],
        out_specs=[
            pl.BlockSpec((tm * r, LANES), lambda i: (i, 0)),
            pl.BlockSpec((tm, LANES), lambda i: (i, 0)),
            pl.BlockSpec((1, LANES), lambda i: (0, 0)),
        ],
        out_shape=[jax.ShapeDtypeStruct((m * r, LANES), jnp.uint32),
                   jax.ShapeDtypeStruct((m, LANES), F32),
                   jax.ShapeDtypeStruct((1, LANES), F32)],
        scratch_shapes=[pltpu.VMEM((1, LANES), F32)],
        compiler_params=_params("arbitrary"),
        name="router",
    )(x, gain, wr_p)


def _aligned(v, mult):
    return v if isinstance(v, int) else pl.multiple_of(v, mult)


def _moe_ffn_kernel(tile_expert_ref, tile_rows_ref, n_valid_ref, hp_hbm, meta_hbm, wg_ref, wu_ref, wd_ref,
                    y_hbm, xbuf, ybuf, meta_smem, gsem, ssem, msem, *, tm, r, chunk, unroll):
    del tile_expert_ref
    t = pl.program_id(0)
    n_valid = n_valid_ref[0]

    def meta_copy(tile):
        slot = tile % 3
        return pltpu.make_async_copy(meta_hbm.at[pl.ds(_aligned(tile * chunk, chunk), chunk)],
                                     meta_smem.at[pl.ds(_aligned(slot * chunk, chunk), chunk)], msem.at[slot])

    def rows(ref, row, n=1):
        return ref.at[pl.ds(_aligned(row * r, r), n * r), :]

    def gather_copy(tile, row):
        tok = meta_smem[(tile % 3) * chunk + row]
        return pltpu.make_async_copy(rows(hp_hbm, tok), rows(xbuf.at[tile % 2], row), gsem.at[tile % 2])

    def scatter_copy(tile, row):
        dst = meta_smem[(tile % 3) * chunk + tm + row]
        return pltpu.make_async_copy(rows(ybuf.at[tile % 2], row), rows(y_hbm, dst), ssem.at[tile % 2])

    def gather_wait(tile, n):
        pltpu.make_async_copy(rows(hp_hbm, 0, n), rows(xbuf.at[tile % 2], 0, n), gsem.at[tile % 2]).wait()

    def scatter_wait(tile, n):
        pltpu.make_async_copy(rows(ybuf.at[tile % 2], 0, n), rows(y_hbm, 0, n), ssem.at[tile % 2]).wait()

    def for_rows(fn, n=None):
        def body(row, carry):
            fn(row)
            return carry

        if n is None:
            lax.fori_loop(0, tm, body, 0, unroll=unroll)
            return

        def group(g, carry):
            for j in range(unroll):
                fn(g * unroll + j)
            return carry

        lax.fori_loop(0, n // unroll, group, 0)
        lax.fori_loop((n // unroll) * unroll, n, body, 0)

    def wait_rows(wait, n=None):
        def many(g, carry):
            wait(unroll)
            return carry

        def one(g, carry):
            wait(1)
            return carry

        if n is None:
            lax.fori_loop(0, tm // unroll, many, 0)
            return
        lax.fori_loop(0, n // unroll, many, 0)
        lax.fori_loop(0, n % unroll, one, 0)

    def valid_rows(tile):
        return tile_rows_ref[tile]

    @pl.when(t == 0)
    def _():
        meta_copy(0).start()

        @pl.when(n_valid > 1)
        def _():
            meta_copy(1).start()

        meta_copy(0).wait()
        for_rows(lambda row: gather_copy(0, row).start())

    @pl.when(t + 2 < n_valid)
    def _():
        meta_copy(t + 2).start()

    @pl.when(t + 1 < n_valid)
    def _():
        meta_copy(t + 1).wait()
        for_rows(lambda row: gather_copy(t + 1, row).start())

    @pl.when(t < n_valid)
    def _():
        wait_rows(functools.partial(gather_wait, t))

        @pl.when(t >= 2)
        def _():
            wait_rows(functools.partial(scatter_wait, t - 2), valid_rows(t - 2))

        his, los = _unpack_rows(xbuf.at[t % 2], tm, r)
        x = jnp.concatenate([p.astype(BF16) for p in his + los], axis=1)
        act = jax.nn.silu(_dot(x, wg_ref[0])) * _dot(x, wu_ref[0])
        _pack_rows(_dot(act.astype(BF16), wd_ref[0]), ybuf.at[t % 2])
        for_rows(lambda row: scatter_copy(t, row).start(), valid_rows(t))

    @pl.when(t == n_valid - 1)
    def _():
        @pl.when(t >= 1)
        def _():
            wait_rows(functools.partial(scatter_wait, t - 1), valid_rows(t - 1))

        wait_rows(functools.partial(scatter_wait, t), valid_rows(t))


def _combine_kernel(x_ref, route_ref, y0_ref, y1_ref, o_ref, *, r):
    tm, d = x_ref.shape
    half = d // 2
    g0 = route_ref[:, 2:3]
    g1 = route_ref[:, 3:4]
    h0, l0 = _unpack_rows(y0_ref, tm, r)
    h1, l1 = _unpack_rows(y1_ref, tm, r)
    for j in range(r):
        a = slice(j * LANES, (j + 1) * LANES)
        b = slice(half + j * LANES, half + (j + 1) * LANES)
        o_ref[:, a] = x_ref[:, a] + (g0 * h0[j] + g1 * h1[j])
        o_ref[:, b] = x_ref[:, b] + (g0 * l0[j] + g1 * l1[j])


def _moe(x, gain, wr, wg, wu, wd):
    m, d = x.shape
    n_experts, _, f = wg.shape
    r = d // (2 * LANES)
    hp, route, counts = _router(x, gain, wr)

    tm = _pick(m, min(512, max(256, m // (2 * n_experts))))
    n_tiles = (2 * m) // tm + n_experts
    n_rows = n_tiles * tm
    expert = route[:, 0:2].astype(jnp.int32)
    rank = route[:, 4:6].astype(jnp.int32)
    cnt = counts[0, :n_experts].astype(jnp.int32)
    tiles_e = (cnt + tm - 1) // tm
    tile_end = jnp.cumsum(tiles_e)
    n_valid = tile_end[-1]
    tile_start = tile_end - tiles_e
    hot = expert[..., None] == jnp.arange(n_experts, dtype=jnp.int32)
    pos = (jnp.sum(jnp.where(hot, tile_start, 0), axis=-1) * tm + rank).reshape(-1)
    row_assign = jnp.zeros((n_rows,), jnp.int32).at[pos].set(jnp.arange(2 * m, dtype=jnp.int32))
    row_token = (row_assign >> 1).reshape(n_tiles, tm)
    row_dest = ((row_assign & 1) * m + (row_assign >> 1)).reshape(n_tiles, tm)
    chunk = -(-2 * tm // SMEM_CHUNK) * SMEM_CHUNK
    meta = jnp.pad(jnp.concatenate([row_token, row_dest], axis=1), ((0, 0), (0, chunk - 2 * tm))).reshape(-1)
    tile_ids = jnp.arange(n_tiles, dtype=jnp.int32)
    tile_expert = jnp.searchsorted(tile_end, jnp.minimum(tile_ids, n_valid - 1), side="right").astype(jnp.int32)
    tile_rows = jnp.clip(cnt[tile_expert] - (tile_ids - tile_start[tile_expert]) * tm, 0, tm)
    tile_rows = jnp.where(tile_ids < n_valid, tile_rows, 0).astype(jnp.int32)

    y = pl.pallas_call(
        functools.partial(_moe_ffn_kernel, tm=tm, r=r, chunk=chunk, unroll=8),
        grid_spec=pltpu.PrefetchScalarGridSpec(
            num_scalar_prefetch=3,
            grid=(n_tiles,),
            in_specs=[
                pl.BlockSpec(memory_space=pl.ANY),
                pl.BlockSpec(memory_space=pl.ANY),
                pl.BlockSpec((1, d, f), lambda t, te, tr, nv: (te[t], 0, 0)),
                pl.BlockSpec((1, d, f), lambda t, te, tr, nv: (te[t], 0, 0)),
                pl.BlockSpec((1, f, d), lambda t, te, tr, nv: (te[t], 0, 0)),
            ],
            out_specs=pl.BlockSpec(memory_space=pl.ANY),
            scratch_shapes=[
                pltpu.VMEM((2, tm * r, LANES), jnp.uint32),
                pltpu.VMEM((2, tm * r, LANES), jnp.uint32),
                pltpu.SMEM((3 * chunk,), jnp.int32),
                pltpu.SemaphoreType.DMA((2,)),
                pltpu.SemaphoreType.DMA((2,)),
                pltpu.SemaphoreType.DMA((3,)),
            ],
        ),
        out_shape=jax.ShapeDtypeStruct((2 * m * r, LANES), jnp.uint32),
        compiler_params=_params("arbitrary"),
        name="moe_ffn",
    )(tile_expert, tile_rows, n_valid.reshape(1), hp, meta, wg, wu, wd)

    tc = _pick(m, 512)
    nb = m // tc
    return pl.pallas_call(
        functools.partial(_combine_kernel, r=r),
        grid=(nb,),
        in_specs=[
            pl.BlockSpec((tc, d), lambda i: (i, 0)),
            pl.BlockSpec((tc, LANES), lambda i: (i, 0)),
            pl.BlockSpec((tc * r, LANES), lambda i: (i, 0)),
            pl.BlockSpec((tc * r, LANES), lambda i: (nb + i, 0)),
        ],
        out_specs=pl.BlockSpec((tc, d), lambda i: (i, 0)),
        out_shape=jax.ShapeDtypeStruct((m, d), F32),
        compiler_params=_params("parallel"),
        name="moe_combine",
    )(x, route, y, y)


def _trunk(x, batch, seq, conv_prev, cache, w):
    d = x.shape[1]
    depth = w["norm_mix"].shape[0]
    n_conv = w["conv_w_dw"].shape[0]
    width = w["conv_w_dw"].shape[1]
    scale = LOG2_E * float(LANES) ** -0.5
    states = []
    k_f32 = v_f32 = k_b = v_b = None
    for layer in range(depth):
        if layer == n_conv:
            k_f32, v_f32, k_b, v_b = _kv_proj(x, w["kv_norm"], w["wk"], w["wv"], w["k_norm"])
        gain = w["norm_mix"][layer:layer + 1]
        if layer < n_conv:
            u = _glu_in(x, gain, w["conv_wa"][layer], w["conv_wg"][layer],
                        w["conv_ba"][layer], w["conv_bg"][layer])
            prev = conv_prev[layer]
            u3 = u.reshape(batch, seq, d)
            keep = width - 1
            if seq >= keep:
                states.append(u3[:, seq - keep:])
            else:
                states.append(jnp.concatenate([prev[:, seq:], u3], axis=1))
            prev_p = jnp.pad(prev, ((0, 0), (HALO - keep, 0), (0, 0)))
            c = _conv_ln_swish(u, prev_p, w["conv_w_dw"][layer], w["conv_b_dw"][layer:layer + 1],
                               w["conv_ln_g"][layer:layer + 1], w["conv_ln_b"][layer:layer + 1], batch, seq)
            x = _mm_res(c, w["conv_w_out"][layer], w["conv_b_out"][layer:layer + 1], x)
        else:
            j = layer - n_conv
            q = _q_proj(x, gain, w["w_q"][j], w["q_norm"][j:j + 1], scale)
            if cache is None:
                o = _attn_self(q, k_b, v_b, batch, seq)
            else:
                o = _attn_cached(q, k_b, v_b, cache[0], cache[1], batch, seq)
            x = _mm_res(o, w["w_o"][j], jnp.zeros((1, d), F32), x)
        gain = w["norm_ffn"][layer:layer + 1]
        i = layer // 2
        if layer % 2 == 0:
            x = _ffn(x, gain, w["ffn_w_gate"][i], w["ffn_w_up"][i], w["ffn_w_down"][i])
        else:
            x = _moe(x, gain, w["moe_router"][i], w["moe_w_gate"][i], w["moe_w_up"][i], w["moe_w_down"][i])
    return x, k_f32, v_f32, jnp.stack(states)


def kernel(x_prompt, x_sample, cache_k, cache_v, state_conv, norm_mix, norm_ffn, conv_w_in, conv_b_in, conv_w_dw, conv_b_dw, conv_ln_g, conv_ln_b, conv_w_out, conv_b_out, kv_norm, w_kv, k_norm, w_q, q_norm, w_o, ffn_w_gate, ffn_w_up, ffn_w_down, moe_router, moe_w_gate, moe_w_up, moe_w_down):
    d = x_prompt.shape[-1]
    heads = d // LANES
    assert k_norm.shape[0] == LANES and w_kv.shape[1] == 2 * d
    w = {
        "norm_mix": norm_mix, "norm_ffn": norm_ffn,
        "conv_wa": conv_w_in[:, :, :d].astype(BF16), "conv_wg": conv_w_in[:, :, d:].astype(BF16),
        "conv_ba": conv_b_in[:, None, :d], "conv_bg": conv_b_in[:, None, d:],
        "conv_w_dw": conv_w_dw, "conv_b_dw": conv_b_dw, "conv_ln_g": conv_ln_g, "conv_ln_b": conv_ln_b,
        "conv_w_out": conv_w_out.astype(BF16), "conv_b_out": conv_b_out,
        "kv_norm": kv_norm[None], "wk": w_kv[:, :d].astype(BF16), "wv": w_kv[:, d:].astype(BF16),
        "k_norm": k_norm[None], "w_q": w_q.astype(BF16), "q_norm": q_norm, "w_o": w_o.astype(BF16),
        "ffn_w_gate": ffn_w_gate.astype(BF16), "ffn_w_up": ffn_w_up.astype(BF16),
        "ffn_w_down": ffn_w_down.astype(BF16),
        "moe_router": moe_router, "moe_w_gate": moe_w_gate.astype(BF16),
        "moe_w_up": moe_w_up.astype(BF16), "moe_w_down": moe_w_down.astype(BF16),
    }
    n_conv, width = conv_w_dw.shape[0], conv_w_dw.shape[1]

    bp, sp, _ = x_prompt.shape
    zero_prev = jnp.zeros((n_conv, bp, width - 1, d), x_prompt.dtype)
    y_p, k_p, v_p, conv_p = _trunk(x_prompt.reshape(bp * sp, d), bp, sp, zero_prev, None, w)

    bs, ss, _ = x_sample.shape
    y_s, k_s, v_s, conv_s = _trunk(x_sample.reshape(bs * ss, d), bs, ss, state_conv, (cache_k, cache_v), w)

    return (y_p.reshape(bp, sp, d), y_s.reshape(bs, ss, d),
            k_p.reshape(bp, sp, heads, LANES), v_p.reshape(bp, sp, heads, LANES), conv_p,
            k_s.reshape(bs, ss, heads, LANES), v_s.reshape(bs, ss, heads, LANES), conv_s)
```

```python
import functools

import jax
import jax.numpy as jnp
from jax import lax
from jax.experimental import pallas as pl
from jax.experimental.pallas import tpu as pltpu

F32 = jnp.float32
BF16 = jnp.bfloat16

EPS = 1e-6
LANES = 128
HALO = 32
SMEM_CHUNK = 1024
VMEM_LIMIT = 56 * 1024 * 1024
LOG2_E = 1.4426950408889634
LOG_ZERO = -151.0


def _params(*sem):
    return pltpu.CompilerParams(dimension_semantics=sem, vmem_limit_bytes=VMEM_LIMIT)


def _pick(n, pref, mult=8):
    if n <= pref:
        return n
    t = (pref // mult) * mult
    while t > mult and n % t:
        t -= mult
    assert n % t == 0, (n, pref, mult)
    return t


def _rms_bf16(x, g):
    ms = jnp.mean(x * x, axis=-1, keepdims=True)
    return (x * lax.rsqrt(ms + EPS) * g).astype(BF16)


def _dot(a, b):
    return jnp.dot(a, b, preferred_element_type=F32)


def _glu_in_kernel(x_ref, g_ref, wa_ref, wg_ref, ba_ref, bg_ref, u_ref, h_scr):
    @pl.when(pl.program_id(1) == 0)
    def _():
        h_scr[...] = _rms_bf16(x_ref[...], g_ref[...])

    h = h_scr[...]
    a = _dot(h, wa_ref[...]) + ba_ref[...]
    g = _dot(h, wg_ref[...]) + bg_ref[...]
    u_ref[...] = a * jax.nn.sigmoid(g)


def _glu_in(x, gain, wa, wg, ba, bg):
    m, d = x.shape
    n = wa.shape[1]
    tm, tn = _pick(m, 1024), _pick(n, 1024, LANES)
    return pl.pallas_call(
        _glu_in_kernel,
        grid=(m // tm, n // tn),
        in_specs=[
            pl.BlockSpec((tm, d), lambda i, j: (i, 0)),
            pl.BlockSpec((1, d), lambda i, j: (0, 0)),
            pl.BlockSpec((d, tn), lambda i, j: (0, j)),
            pl.BlockSpec((d, tn), lambda i, j: (0, j)),
            pl.BlockSpec((1, tn), lambda i, j: (0, j)),
            pl.BlockSpec((1, tn), lambda i, j: (0, j)),
        ],
        out_specs=pl.BlockSpec((tm, tn), lambda i, j: (i, j)),
        out_shape=jax.ShapeDtypeStruct((m, n), F32),
        scratch_shapes=[pltpu.VMEM((tm, d), BF16)],
        compiler_params=_params("parallel", "arbitrary"),
        name="glu_in",
    )(x, gain, wa, wg, ba, bg)


def _conv_kernel(u_ref, halo_ref, prev_ref, wdw_ref, bdw_ref, lng_ref, lnb_ref, c_ref,
                 ext_scr, sh_scr, acc_scr, *, tc, width, rows_ln):
    d = u_ref.shape[1]
    i = pl.program_id(1)

    @pl.when(i == 0)
    def _():
        ext_scr[0:HALO, :] = prev_ref[0]

    @pl.when(i > 0)
    def _():
        ext_scr[0:HALO, :] = halo_ref[...]

    ext_scr[HALO:HALO + tc, :] = u_ref[...]

    off = HALO - (width - 1)
    n_shift = 8
    rc = min(tc, 64)

    def lane_body(c, carry):
        l0 = pl.multiple_of(c * LANES, LANES)
        for r in range(n_shift):
            taps_r = [w for w in range(width) if w % n_shift == r]
            n_rows = tc + (max(taps_r) - r)
            sh_scr[r, 0:n_rows, :] = ext_scr[off + r:off + r + n_rows, pl.ds(l0, LANES)]
        for r0 in range(0, tc, rc):
            acc = jnp.broadcast_to(bdw_ref[:, pl.ds(l0, LANES)], (rc, LANES))
            for w in range(width):
                r, q = w % n_shift, w // n_shift
                win = sh_scr[r, r0 + 8 * q:r0 + 8 * q + rc, :]
                acc = acc + win * wdw_ref[w:w + 1, pl.ds(l0, LANES)]
            acc_scr[r0:r0 + rc, pl.ds(l0, LANES)] = acc
        return carry

    lax.fori_loop(0, d // LANES, lane_body, 0)

    def ln_body(r, carry):
        r0 = pl.multiple_of(r * rows_ln, rows_ln)
        c = acc_scr[pl.ds(r0, rows_ln), :]
        mu = jnp.mean(c, axis=-1, keepdims=True)
        xc = c - mu
        var = jnp.mean(xc * xc, axis=-1, keepdims=True)
        y = xc * lax.rsqrt(var + EPS) * lng_ref[...] + lnb_ref[...]
        c_ref[pl.ds(r0, rows_ln), :] = (y * jax.nn.sigmoid(y)).astype(c_ref.dtype)
        return carry

    lax.fori_loop(0, tc // rows_ln, ln_body, 0, unroll=min(4, tc // rows_ln))


def _conv_ln_swish(u, prev, wdw, bdw, lng, lnb, batch, seq):
    m, d = u.shape
    width = wdw.shape[0]
    assert width - 1 <= HALO and seq % HALO == 0
    tc = _pick(seq, 256, HALO)
    nt = seq // tc
    hb = tc // HALO
    wdw_p = jnp.pad(wdw, ((0, (-width) % 8), (0, 0)))
    kern = functools.partial(_conv_kernel, tc=tc, width=width, rows_ln=16)
    return pl.pallas_call(
        kern,
        grid=(batch, nt),
        in_specs=[
            pl.BlockSpec((tc, d), lambda b, i: (b * nt + i, 0)),
            pl.BlockSpec((HALO, d), lambda b, i: (jnp.maximum((b * nt + i) * hb - 1, 0), 0)),
            pl.BlockSpec((1, HALO, d), lambda b, i: (b, 0, 0)),
            pl.BlockSpec(wdw_p.shape, lambda b, i: (0, 0)),
            pl.BlockSpec((1, d), lambda b, i: (0, 0)),
            pl.BlockSpec((1, d), lambda b, i: (0, 0)),
            pl.BlockSpec((1, d), lambda b, i: (0, 0)),
        ],
        out_specs=pl.BlockSpec((tc, d), lambda b, i: (b * nt + i, 0)),
        out_shape=jax.ShapeDtypeStruct((m, d), BF16),
        scratch_shapes=[
            pltpu.VMEM((HALO + tc, d), F32),
            pltpu.VMEM((8, tc + 24, LANES), F32),
            pltpu.VMEM((tc, d), F32),
        ],
        compiler_params=_params("parallel", "arbitrary"),
        name="conv_ln_swish",
    )(u, u, prev, wdw_p, bdw, lng, lnb)


def _mm_res_kernel(x_ref, w_ref, b_ref, r_ref, o_ref):
    o_ref[...] = r_ref[...] + b_ref[...] + _dot(x_ref[...], w_ref[...])


def _mm_res(x, w, b, res):
    m, k = x.shape
    n = w.shape[1]
    tm, tn = _pick(m, 1024), _pick(n, 1024, LANES)
    return pl.pallas_call(
        _mm_res_kernel,
        grid=(m // tm, n // tn),
        in_specs=[
            pl.BlockSpec((tm, k), lambda i, j: (i, 0)),
            pl.BlockSpec((k, tn), lambda i, j: (0, j)),
            pl.BlockSpec((1, tn), lambda i, j: (0, j)),
            pl.BlockSpec((tm, tn), lambda i, j: (i, j)),
        ],
        out_specs=pl.BlockSpec((tm, tn), lambda i, j: (i, j)),
        out_shape=jax.ShapeDtypeStruct((m, n), F32),
        compiler_params=_params("parallel", "parallel"),
        name="mm_res",
    )(x, w, b, res)


def _head_rms(y, gain, scale):
    out = []
    for c in range(y.shape[1] // LANES):
        yc = y[:, c * LANES:(c + 1) * LANES]
        ms = jnp.mean(yc * yc, axis=-1, keepdims=True)
        yc = yc * lax.rsqrt(ms + EPS) * gain
        out.append(yc * scale if scale != 1.0 else yc)
    return out


def _q_kernel(x_ref, g_ref, w_ref, qn_ref, q_ref, h_scr, *, scale):
    @pl.when(pl.program_id(1) == 0)
    def _():
        h_scr[...] = _rms_bf16(x_ref[...], g_ref[...])

    y = _dot(h_scr[...], w_ref[...])
    for c, yc in enumerate(_head_rms(y, qn_ref[...], scale)):
        q_ref[:, c * LANES:(c + 1) * LANES] = yc.astype(q_ref.dtype)


def _q_proj(x, gain, w, qn, scale):
    m, d = x.shape
    n = w.shape[1]
    tm, tn = _pick(m, 1024), _pick(n, 1024, LANES)
    return pl.pallas_call(
        functools.partial(_q_kernel, scale=scale),
        grid=(m // tm, n // tn),
        in_specs=[
            pl.BlockSpec((tm, d), lambda i, j: (i, 0)),
            pl.BlockSpec((1, d), lambda i, j: (0, 0)),
            pl.BlockSpec((d, tn), lambda i, j: (0, j)),
            pl.BlockSpec((1, LANES), lambda i, j: (0, 0)),
        ],
        out_specs=pl.BlockSpec((tm, tn), lambda i, j: (i, j)),
        out_shape=jax.ShapeDtypeStruct((m, n), BF16),
        scratch_shapes=[pltpu.VMEM((tm, d), BF16)],
        compiler_params=_params("parallel", "arbitrary"),
        name="q_proj",
    )(x, gain, w, qn)


def _kv_kernel(x_ref, g_ref, wk_ref, wv_ref, kn_ref, k_ref, v_ref, kb_ref, vb_ref, h_scr):
    @pl.when(pl.program_id(1) == 0)
    def _():
        h_scr[...] = _rms_bf16(x_ref[...], g_ref[...])

    h = h_scr[...]
    k = _dot(h, wk_ref[...])
    for c, kc in enumerate(_head_rms(k, kn_ref[...], 1.0)):
        k_ref[:, c * LANES:(c + 1) * LANES] = kc
        kb_ref[:, c * LANES:(c + 1) * LANES] = kc.astype(BF16)
    v = _dot(h, wv_ref[...])
    v_ref[...] = v
    vb_ref[...] = v.astype(BF16)


def _kv_proj(x, gain, wk, wv, kn):
    m, d = x.shape
    n = wk.shape[1]
    tm, tn = _pick(m, 1024), _pick(n, 512, LANES)
    blk = pl.BlockSpec((tm, tn), lambda i, j: (i, j))
    return pl.pallas_call(
        _kv_kernel,
        grid=(m // tm, n // tn),
        in_specs=[
            pl.BlockSpec((tm, d), lambda i, j: (i, 0)),
            pl.BlockSpec((1, d), lambda i, j: (0, 0)),
            pl.BlockSpec((d, tn), lambda i, j: (0, j)),
            pl.BlockSpec((d, tn), lambda i, j: (0, j)),
            pl.BlockSpec((1, LANES), lambda i, j: (0, 0)),
        ],
        out_specs=[blk, blk, blk, blk],
        out_shape=[jax.ShapeDtypeStruct((m, n), F32), jax.ShapeDtypeStruct((m, n), F32),
                   jax.ShapeDtypeStruct((m, n), BF16), jax.ShapeDtypeStruct((m, n), BF16)],
        scratch_shapes=[pltpu.VMEM((tm, d), BF16)],
        compiler_params=_params("parallel", "arbitrary"),
        name="kv_proj",
    )(x, gain, wk, wv, kn)


def _strict_lower(n):
    row = lax.broadcasted_iota(jnp.int32, (n, n), 0)
    col = lax.broadcasted_iota(jnp.int32, (n, n), 1)
    return col < row


def _sb_block(q, k, v, tri, later, mask):
    z = lax.dot_general(q, k, (((1,), (1,)), ((), ())), preferred_element_type=F32)
    drop = jnp.maximum(z, 0.0) + LOG2_E * jnp.log(1.0 + jnp.exp2(-jnp.abs(z)))
    if mask is not None:
        drop = jnp.where(mask, drop, 0.0)
    in_blk = _dot(drop.astype(BF16), tri)
    w = jnp.exp2((z - drop) - in_blk + later)
    if mask is not None:
        w = jnp.where(mask, w, 0.0)
    o = _dot(w.astype(BF16), v)
    return o, later - jnp.sum(drop, axis=-1, keepdims=True)


def _sb_sweep(chains, tri_diag, tri, q_ref, acc_scr):
    tq = chains[0][0].stop - chains[0][0].start
    mask = _strict_lower(tq)
    laters = []
    for rows, lanes, diag_kv, _, _ in chains:
        k, v = diag_kv()
        o, later = _sb_block(q_ref[rows, lanes], k, v, tri_diag, jnp.zeros((tq, 1), F32), mask)
        acc_scr[rows, lanes] = o
        laters.append(later)

    def live(j, chain, later):
        n = chain[4]
        return later if isinstance(n, int) else jnp.where(j < n, later, 2 * LOG_ZERO)

    n_max = functools.reduce(jnp.maximum, [c[4] for c in chains])

    def cond(c):
        j = c[0]
        worst = functools.reduce(jnp.maximum, [live(j, ch, lt) for ch, lt in zip(chains, c[1:])])
        return jnp.logical_and(j < n_max, jnp.max(worst) > LOG_ZERO)

    def body(c):
        j = c[0]
        out = [j + 1]
        for chain, later in zip(chains, c[1:]):
            rows, lanes, _, earlier_kv, n = chain
            k, v = earlier_kv(j)
            do, later = _sb_block(q_ref[rows, lanes], k, v, tri, live(j, chain, later), None)
            acc_scr[rows, lanes] += do
            out.append(later)
        return tuple(out)

    lax.while_loop(cond, body, (jnp.int32(0), *laters))


def _attn_self_kernel(q_ref, k_ref, v_ref, o_ref, acc_scr, *, bk, n_sub):
    i = pl.program_id(2)
    tri = jnp.where(_strict_lower(bk), 1.0, 0.0).astype(BF16)

    def load_kv(lanes, kb):
        k0 = pl.multiple_of(kb * bk, bk)
        return k_ref[pl.ds(k0, bk), lanes], v_ref[pl.ds(k0, bk), lanes]

    chains = []
    for s in range(n_sub):
        qb = i * n_sub + s
        for h in range(q_ref.shape[1] // LANES):
            lanes = slice(h * LANES, (h + 1) * LANES)
            chains.append((slice(s * bk, (s + 1) * bk), lanes,
                           functools.partial(load_kv, lanes, qb),
                           lambda j, lanes=lanes, qb=qb: load_kv(lanes, jnp.maximum(qb - 1 - j, 0)),
                           qb))
    _sb_sweep(chains, tri, tri, q_ref, acc_scr)
    o_ref[...] = acc_scr[...].astype(o_ref.dtype)


def _attn_self(q, k, v, batch, seq):
    m, hd = q.shape
    hb = _pick(hd, 4 * LANES, LANES)
    bk = _pick(seq, 256, LANES)
    n_sub = 2 if seq % (2 * bk) == 0 else 1
    bq = n_sub * bk
    nq = seq // bq
    kv_spec = pl.BlockSpec((seq, hb), lambda b, h, i: (b, h), pipeline_mode=pl.Buffered(1))
    return pl.pallas_call(
        functools.partial(_attn_self_kernel, bk=bk, n_sub=n_sub),
        grid=(batch, hd // hb, nq),
        in_specs=[pl.BlockSpec((bq, hb), lambda b, h, i: (b * nq + i, h)), kv_spec, kv_spec],
        out_specs=pl.BlockSpec((bq, hb), lambda b, h, i: (b * nq + i, h)),
        out_shape=jax.ShapeDtypeStruct((m, hd), BF16),
        scratch_shapes=[pltpu.VMEM((bq, hb), F32)],
        compiler_params=_params("parallel", "parallel", "arbitrary"),
        name="attn_self",
    )(q, k, v)


def _attn_cached_kernel(q_ref, kn_ref, vn_ref, ck_ref, cv_ref, o_ref, acc_scr, *, bk):
    tq = q_ref.shape[0]
    n_heads = q_ref.shape[1] // LANES
    n_old = ck_ref.shape[1]
    tri = jnp.where(_strict_lower(bk), 1.0, 0.0).astype(BF16)

    def diag_kv(lanes):
        return kn_ref[:, lanes], vn_ref[:, lanes]

    def earlier_kv(h, j):
        rows = pl.ds(h, bk, stride=n_heads)
        return (ck_ref[0, n_old - 1 - j, rows, :].astype(BF16), cv_ref[0, n_old - 1 - j, rows, :].astype(BF16))

    chains = []
    for h in range(n_heads):
        lanes = slice(h * LANES, (h + 1) * LANES)
        chains.append((slice(0, tq), lanes, functools.partial(diag_kv, lanes),
                       functools.partial(earlier_kv, h), n_old))
    _sb_sweep(chains, tri[:tq, :tq], tri, q_ref, acc_scr)
    o_ref[...] = acc_scr[...].astype(o_ref.dtype)


def _attn_cached(q, k_new, v_new, cache_k, cache_v, batch, seq):
    m, hd = q.shape
    _, past, n_heads, _ = cache_k.shape
    assert n_heads * LANES == hd
    bk = _pick(past, 256, LANES)
    assert seq <= bk and seq % 16 == 0
    blocked = (batch, past // bk, bk * n_heads, LANES)
    new = pl.BlockSpec((seq, hd), lambda b: (b, 0))
    old = pl.BlockSpec((1,) + blocked[1:], lambda b: (b, 0, 0, 0))
    return pl.pallas_call(
        functools.partial(_attn_cached_kernel, bk=bk),
        grid=(batch,),
        in_specs=[new, new, new, old, old],
        out_specs=new,
        out_shape=jax.ShapeDtypeStruct((m, hd), BF16),
        scratch_shapes=[pltpu.VMEM((seq, hd), F32)],
        compiler_params=_params("parallel"),
        name="attn_cached",
    )(q, k_new, v_new, cache_k.reshape(blocked), cache_v.reshape(blocked))


def _ffn_kernel(x_ref, g_ref, wg_ref, wu_ref, wd_ref, o_ref, h_scr):
    @pl.when(pl.program_id(1) == 0)
    def _():
        x = x_ref[...]
        h_scr[...] = _rms_bf16(x, g_ref[...])
        o_ref[...] = x

    h = h_scr[...]
    act = jax.nn.silu(_dot(h, wg_ref[...])) * _dot(h, wu_ref[...])
    o_ref[...] += _dot(act.astype(BF16), wd_ref[...])


def _ffn(x, gain, wg, wu, wd):
    m, d = x.shape
    f = wg.shape[1]
    tm, tf = _pick(m, 1024), _pick(f, 256, LANES)
    return pl.pallas_call(
        _ffn_kernel,
        grid=(m // tm, f // tf),
        in_specs=[
            pl.BlockSpec((tm, d), lambda i, j: (i, 0)),
            pl.BlockSpec((1, d), lambda i, j: (0, 0)),
            pl.BlockSpec((d, tf), lambda i, j: (0, j)),
            pl.BlockSpec((d, tf), lambda i, j: (0, j)),
            pl.BlockSpec((tf, d), lambda i, j: (j, 0)),
        ],
        out_specs=pl.BlockSpec((tm, d), lambda i, j: (i, 0)),
        out_shape=jax.ShapeDtypeStruct((m, d), F32),
        scratch_shapes=[pltpu.VMEM((tm, d), BF16)],
        compiler_params=_params("parallel", "arbitrary"),
        name="ffn",
    )(x, gain, wg, wu, wd)


def _split_bf16(a):
    hi = a.astype(BF16)
    return hi, (a - hi.astype(F32)).astype(BF16)


def _pack_rows(y, dst_ref):
    tm, d = y.shape
    r = d // (2 * LANES)
    half = d // 2
    for j in range(r):
        hi = pltpu.bitcast(y[:, j * LANES:(j + 1) * LANES].astype(BF16).astype(F32), jnp.uint32)
        lo = pltpu.bitcast(y[:, half + j * LANES:half + (j + 1) * LANES].astype(BF16).astype(F32), jnp.uint32)
        dst_ref[pl.ds(j, tm, stride=r), :] = hi | (lo >> 16)


def _unpack_rows(src_ref, tm, r):
    his, los = [], []
    for j in range(r):
        p = src_ref[pl.ds(j, tm, stride=r), :]
        his.append(pltpu.bitcast(p & jnp.uint32(0xFFFF0000), F32))
        los.append(pltpu.bitcast(p << 16, F32))
    return his, los


def _router_kernel(x_ref, g_ref, wr_ref, hp_ref, route_ref, counts_ref, run_scr, *, n_experts):
    @pl.when(pl.program_id(0) == 0)
    def _():
        run_scr[...] = jnp.zeros_like(run_scr)

    x = x_ref[...]
    tm = x.shape[0]
    ms = jnp.mean(x * x, axis=-1, keepdims=True)
    h = x * lax.rsqrt(ms + EPS) * g_ref[...]
    _pack_rows(h, hp_ref)
    h_hi, h_lo = _split_bf16(h)
    w_hi, w_lo = _split_bf16(wr_ref[...])
    logits = _dot(h_hi, w_hi) + (_dot(h_hi, w_lo) + _dot(h_lo, w_hi))
    lane = lax.broadcasted_iota(jnp.int32, logits.shape, 1).astype(F32)
    neg = -jnp.inf
    l1 = jnp.where(lane < n_experts, logits, neg)
    m1 = jnp.max(l1, axis=-1, keepdims=True)
    i1 = jnp.min(jnp.where(l1 == m1, lane, float(LANES)), axis=-1, keepdims=True)
    l2 = jnp.where(lane == i1, neg, l1)
    m2 = jnp.max(l2, axis=-1, keepdims=True)
    i2 = jnp.min(jnp.where(l2 == m2, lane, float(LANES)), axis=-1, keepdims=True)
    e2 = jnp.exp(m2 - m1)
    den = 1.0 + e2
    hot1 = lane == i1
    hot2 = lane == i2
    both = jnp.where(jnp.logical_or(hot1, hot2), 1.0, 0.0)
    earlier = jnp.where(_strict_lower(tm), 1.0, 0.0).astype(BF16)
    before = run_scr[...] + _dot(earlier, both.astype(BF16))
    rank1 = jnp.sum(jnp.where(hot1, before, 0.0), axis=-1, keepdims=True)
    rank2 = jnp.sum(jnp.where(hot2, before, 0.0), axis=-1, keepdims=True)
    run_scr[...] += jnp.sum(both, axis=0, keepdims=True)
    counts_ref[...] = run_scr[...]
    route = jnp.zeros_like(logits)
    for k, val in enumerate((i1, i2, 1.0 / den, e2 / den, rank1, rank2)):
        route = jnp.where(lane == float(k), val, route)
    route_ref[...] = route


def _router(x, gain, wr):
    m, d = x.shape
    n_experts = wr.shape[1]
    assert n_experts <= LANES and d % (2 * LANES) == 0
    r = d // (2 * LANES)
    wr_p = jnp.pad(wr, ((0, 0), (0, LANES - n_experts)))
    tm = _pick(m, 512)
    return pl.pallas_call(
        functools.partial(_router_kernel, n_experts=n_experts),
        grid=(m // tm,),
        in_specs=[
            pl.BlockSpec((tm, d), lambda i: (i, 0)),
            pl.BlockSpec((1, d), lambda i: (0, 0)),
            pl.BlockSpec((d, LANES), lambda i: (0, 0)),
        ],
        out_specs=[
            pl.BlockSpec((tm * r, LANES), lambda i: (i, 0)),
            pl.BlockSpec((tm, LANES), lambda i: (i, 0)),
            pl.BlockSpec((1, LANES), lambda i: (0, 0)),
        ],
        out_shape=[jax.ShapeDtypeStruct((m * r, LANES), jnp.uint32),
                   jax.ShapeDtypeStruct((m, LANES), F32),
                   jax.ShapeDtypeStruct((1, LANES), F32)],
        scratch_shapes=[pltpu.VMEM((1, LANES), F32)],
        compiler_params=_params("arbitrary"),
        name="router",
    )(x, gain, wr_p)


def _aligned(v, mult):
    return v if isinstance(v, int) else pl.multiple_of(v, mult)


def _moe_ffn_kernel(tile_expert_ref, tile_rows_ref, n_valid_ref, hp_hbm, meta_hbm, wg_ref, wu_ref, wd_ref,
                    y_hbm, xbuf, ybuf, meta_smem, gsem, ssem, msem, *, tm, r, chunk, unroll):
    del tile_expert_ref
    t = pl.program_id(0)
    n_valid = n_valid_ref[0]

    def meta_copy(tile):
        slot = tile % 3
        return pltpu.make_async_copy(meta_hbm.at[pl.ds(_aligned(tile * chunk, chunk), chunk)],
                                     meta_smem.at[pl.ds(_aligned(slot * chunk, chunk), chunk)], msem.at[slot])

    def rows(ref, row, n=1):
        return ref.at[pl.ds(_aligned(row * r, r), n * r), :]

    def gather_copy(tile, row):
        tok = meta_smem[(tile % 3) * chunk + row]
        return pltpu.make_async_copy(rows(hp_hbm, tok), rows(xbuf.at[tile % 2], row), gsem.at[tile % 2])

    def scatter_copy(tile, row):
        dst = meta_smem[(tile % 3) * chunk + tm + row]
        return pltpu.make_async_copy(rows(ybuf.at[tile % 2], row), rows(y_hbm, dst), ssem.at[tile % 2])

    def gather_wait(tile, n):
        pltpu.make_async_copy(rows(hp_hbm, 0, n), rows(xbuf.at[tile % 2], 0, n), gsem.at[tile % 2]).wait()

    def scatter_wait(tile, n):
        pltpu.make_async_copy(rows(ybuf.at[tile % 2], 0, n), rows(y_hbm, 0, n), ssem.at[tile % 2]).wait()

    def for_rows(fn, n=None):
        def body(row, carry):
            fn(row, 0)
            return carry

        def group(g, carry):
            for j in range(unroll):
                fn(g * unroll + j, j % 2)
            return carry

        if n is None:
            lax.fori_loop(0, tm // unroll, group, 0)
            return

        lax.fori_loop(0, n // unroll, group, 0)
        lax.fori_loop((n // unroll) * unroll, n, body, 0)

    def wait_rows(wait, n=None):
        def many(g, carry):
            wait(unroll)
            return carry

        def one(g, carry):
            wait(1)
            return carry

        if n is None:
            lax.fori_loop(0, tm // unroll, many, 0)
            return
        lax.fori_loop(0, n // unroll, many, 0)
        lax.fori_loop(0, n % unroll, one, 0)

    def valid_rows(tile):
        return tile_rows_ref[tile]

    @pl.when(t == 0)
    def _():
        meta_copy(0).start()

        @pl.when(n_valid > 1)
        def _():
            meta_copy(1).start()

        meta_copy(0).wait()
        for_rows(lambda row, lane: gather_copy(0, row).start())

    @pl.when(t + 2 < n_valid)
    def _():
        meta_copy(t + 2).start()

    @pl.when(t + 1 < n_valid)
    def _():
        meta_copy(t + 1).wait()
        for_rows(lambda row, lane: gather_copy(t + 1, row).start())

    @pl.when(t < n_valid)
    def _():
        wait_rows(functools.partial(gather_wait, t))

        @pl.when(t >= 2)
        def _():
            wait_rows(functools.partial(scatter_wait, t - 2), valid_rows(t - 2))

        his, los = _unpack_rows(xbuf.at[t % 2], tm, r)
        x = jnp.concatenate([p.astype(BF16) for p in his + los], axis=1)
        act = jax.nn.silu(_dot(x, wg_ref[0])) * _dot(x, wu_ref[0])
        _pack_rows(_dot(act.astype(BF16), wd_ref[0]), ybuf.at[t % 2])
        for_rows(lambda row, lane: scatter_copy(t, row).start(priority=lane), valid_rows(t))

    @pl.when(t == n_valid - 1)
    def _():
        @pl.when(t >= 1)
        def _():
            wait_rows(functools.partial(scatter_wait, t - 1), valid_rows(t - 1))

        wait_rows(functools.partial(scatter_wait, t), valid_rows(t))


def _combine_kernel(x_ref, route_ref, y0_ref, y1_ref, o_ref, *, r):
    tm, d = x_ref.shape
    half = d // 2
    g0 = route_ref[:, 2:3]
    g1 = route_ref[:, 3:4]
    h0, l0 = _unpack_rows(y0_ref, tm, r)
    h1, l1 = _unpack_rows(y1_ref, tm, r)
    for j in range(r):
        a = slice(j * LANES, (j + 1) * LANES)
        b = slice(half + j * LANES, half + (j + 1) * LANES)
        o_ref[:, a] = x_ref[:, a] + (g0 * h0[j] + g1 * h1[j])
        o_ref[:, b] = x_ref[:, b] + (g0 * l0[j] + g1 * l1[j])


def _moe(x, gain, wr, wg, wu, wd):
    m, d = x.shape
    n_experts, _, f = wg.shape
    r = d // (2 * LANES)
    hp, route, counts = _router(x, gain, wr)

    tm = _pick(m, min(512, max(256, m // (2 * n_experts))))
    n_tiles = (2 * m) // tm + n_experts
    n_rows = n_tiles * tm
    expert = route[:, 0:2].astype(jnp.int32)
    rank = route[:, 4:6].astype(jnp.int32)
    cnt = counts[0, :n_experts].astype(jnp.int32)
    tiles_e = (cnt + tm - 1) // tm
    tile_end = jnp.cumsum(tiles_e)
    n_valid = tile_end[-1]
    tile_start = tile_end - tiles_e
    hot = expert[..., None] == jnp.arange(n_experts, dtype=jnp.int32)
    pos = (jnp.sum(jnp.where(hot, tile_start, 0), axis=-1) * tm + rank).reshape(-1)
    row_assign = jnp.zeros((n_rows,), jnp.int32).at[pos].set(jnp.arange(2 * m, dtype=jnp.int32))
    row_token = (row_assign >> 1).reshape(n_tiles, tm)
    row_dest = ((row_assign & 1) * m + (row_assign >> 1)).reshape(n_tiles, tm)
    chunk = -(-2 * tm // SMEM_CHUNK) * SMEM_CHUNK
    meta = jnp.pad(jnp.concatenate([row_token, row_dest], axis=1), ((0, 0), (0, chunk - 2 * tm))).reshape(-1)
    tile_ids = jnp.arange(n_tiles, dtype=jnp.int32)
    tile_expert = jnp.searchsorted(tile_end, jnp.minimum(tile_ids, n_valid - 1), side="right").astype(jnp.int32)
    tile_rows = jnp.clip(cnt[tile_expert] - (tile_ids - tile_start[tile_expert]) * tm, 0, tm)
    tile_rows = jnp.where(tile_ids < n_valid, tile_rows, 0).astype(jnp.int32)

    y = pl.pallas_call(
        functools.partial(_moe_ffn_kernel, tm=tm, r=r, chunk=chunk, unroll=8),
        grid_spec=pltpu.PrefetchScalarGridSpec(
            num_scalar_prefetch=3,
            grid=(n_tiles,),
            in_specs=[
                pl.BlockSpec(memory_space=pl.ANY),
                pl.BlockSpec(memory_space=pl.ANY),
                pl.BlockSpec((1, d, f), lambda t, te, tr, nv: (te[t], 0, 0)),
                pl.BlockSpec((1, d, f), lambda t, te, tr, nv: (te[t], 0, 0)),
                pl.BlockSpec((1, f, d), lambda t, te, tr, nv: (te[t], 0, 0)),
            ],
            out_specs=pl.BlockSpec(memory_space=pl.ANY),
            scratch_shapes=[
                pltpu.VMEM((2, tm * r, LANES), jnp.uint32),
                pltpu.VMEM((2, tm * r, LANES), jnp.uint32),
                pltpu.SMEM((3 * chunk,), jnp.int32),
                pltpu.SemaphoreType.DMA((2,)),
                pltpu.SemaphoreType.DMA((2,)),
                pltpu.SemaphoreType.DMA((3,)),
            ],
        ),
        out_shape=jax.ShapeDtypeStruct((2 * m * r, LANES), jnp.uint32),
        compiler_params=_params("arbitrary"),
        name="moe_ffn",
    )(tile_expert, tile_rows, n_valid.reshape(1), hp, meta, wg, wu, wd)

    tc = _pick(m, 512)
    nb = m // tc
    return pl.pallas_call(
        functools.partial(_combine_kernel, r=r),
        grid=(nb,),
        in_specs=[
            pl.BlockSpec((tc, d), lambda i: (i, 0)),
            pl.BlockSpec((tc, LANES), lambda i: (i, 0)),
            pl.BlockSpec((tc * r, LANES), lambda i: (i, 0)),
            pl.BlockSpec((tc * r, LANES), lambda i: (nb + i, 0)),
        ],
        out_specs=pl.BlockSpec((tc, d), lambda i: (i, 0)),
        out_shape=jax.ShapeDtypeStruct((m, d), F32),
        compiler_params=_params("parallel"),
        name="moe_combine",
    )(x, route, y, y)


def _trunk(x, batch, seq, conv_prev, cache, w):
    d = x.shape[1]
    depth = w["norm_mix"].shape[0]
    n_conv = w["conv_w_dw"].shape[0]
    width = w["conv_w_dw"].shape[1]
    scale = LOG2_E * float(LANES) ** -0.5
    states = []
    k_f32 = v_f32 = k_b = v_b = None
    for layer in range(depth):
        if layer == n_conv:
            k_f32, v_f32, k_b, v_b = _kv_proj(x, w["kv_norm"], w["wk"], w["wv"], w["k_norm"])
        gain = w["norm_mix"][layer:layer + 1]
        if layer < n_conv:
            u = _glu_in(x, gain, w["conv_wa"][layer], w["conv_wg"][layer],
                        w["conv_ba"][layer], w["conv_bg"][layer])
            prev = conv_prev[layer]
            u3 = u.reshape(batch, seq, d)
            keep = width - 1
            if seq >= keep:
                states.append(u3[:, seq - keep:])
            else:
                states.append(jnp.concatenate([prev[:, seq:], u3], axis=1))
            prev_p = jnp.pad(prev, ((0, 0), (HALO - keep, 0), (0, 0)))
            c = _conv_ln_swish(u, prev_p, w["conv_w_dw"][layer], w["conv_b_dw"][layer:layer + 1],
                               w["conv_ln_g"][layer:layer + 1], w["conv_ln_b"][layer:layer + 1], batch, seq)
            x = _mm_res(c, w["conv_w_out"][layer], w["conv_b_out"][layer:layer + 1], x)
        else:
            j = layer - n_conv
            q = _q_proj(x, gain, w["w_q"][j], w["q_norm"][j:j + 1], scale)
            if cache is None:
                o = _attn_self(q, k_b, v_b, batch, seq)
            else:
                o = _attn_cached(q, k_b, v_b, cache[0], cache[1], batch, seq)
            x = _mm_res(o, w["w_o"][j], jnp.zeros((1, d), F32), x)
        gain = w["norm_ffn"][layer:layer + 1]
        i = layer // 2
        if layer % 2 == 0:
            x = _ffn(x, gain, w["ffn_w_gate"][i], w["ffn_w_up"][i], w["ffn_w_down"][i])
        else:
            x = _moe(x, gain, w["moe_router"][i], w["moe_w_gate"][i], w["moe_w_up"][i], w["moe_w_down"][i])
    return x, k_f32, v_f32, jnp.stack(states)


def kernel(x_prompt, x_sample, cache_k, cache_v, state_conv, norm_mix, norm_ffn, conv_w_in, conv_b_in, conv_w_dw, conv_b_dw, conv_ln_g, conv_ln_b, conv_w_out, conv_b_out, kv_norm, w_kv, k_norm, w_q, q_norm, w_o, ffn_w_gate, ffn_w_up, ffn_w_down, moe_router, moe_w_gate, moe_w_up, moe_w_down):
    d = x_prompt.shape[-1]
    heads = d // LANES
    assert k_norm.shape[0] == LANES and w_kv.shape[1] == 2 * d
    w = {
        "norm_mix": norm_mix, "norm_ffn": norm_ffn,
        "conv_wa": conv_w_in[:, :, :d].astype(BF16), "conv_wg": conv_w_in[:, :, d:].astype(BF16),
        "conv_ba": conv_b_in[:, None, :d], "conv_bg": conv_b_in[:, None, d:],
        "conv_w_dw": conv_w_dw, "conv_b_dw": conv_b_dw, "conv_ln_g": conv_ln_g, "conv_ln_b": conv_ln_b,
        "conv_w_out": conv_w_out.astype(BF16), "conv_b_out": conv_b_out,
        "kv_norm": kv_norm[None], "wk": w_kv[:, :d].astype(BF16), "wv": w_kv[:, d:].astype(BF16),
        "k_norm": k_norm[None], "w_q": w_q.astype(BF16), "q_norm": q_norm, "w_o": w_o.astype(BF16),
        "ffn_w_gate": ffn_w_gate.astype(BF16), "ffn_w_up": ffn_w_up.astype(BF16),
        "ffn_w_down": ffn_w_down.astype(BF16),
        "moe_router": moe_router, "moe_w_gate": moe_w_gate.astype(BF16),
        "moe_w_up": moe_w_up.astype(BF16), "moe_w_down": moe_w_down.astype(BF16),
    }
    n_conv, width = conv_w_dw.shape[0], conv_w_dw.shape[1]

    bp, sp, _ = x_prompt.shape
    zero_prev = jnp.zeros((n_conv, bp, width - 1, d), x_prompt.dtype)
    y_p, k_p, v_p, conv_p = _trunk(x_prompt.reshape(bp * sp, d), bp, sp, zero_prev, None, w)

    bs, ss, _ = x_sample.shape
    y_s, k_s, v_s, conv_s = _trunk(x_sample.reshape(bs * ss, d), bs, ss, state_conv, (cache_k, cache_v), w)

    return (y_p.reshape(bp, sp, d), y_s.reshape(bs, ss, d),
            k_p.reshape(bp, sp, heads, LANES), v_p.reshape(bp, sp, heads, LANES), conv_p,
            k_s.reshape(bs, ss, heads, LANES), v_s.reshape(bs, ss, heads, LANES), conv_s)
```
